```python
import math
import jax, jax.numpy as jnp
from jax import lax
import numpy as np

D_MODEL = 1024
BATCH = 8
SEQ = 4096
DEPTH = 2

N_MIXERS = 2
N_A_LAYERS = (DEPTH + 1) // 2
N_B_LAYERS = DEPTH // 2
RMS_EPS = 1e-6
ROPE_THETA = 500000.0

H_A = 8
DH_A = D_MODEL // H_A
ROT_A = DH_A // 4
H_IDX = 8
D_IDX = 64
ROT_IDX = D_IDX // 4
TOPK_MAX = 256
Q_BLOCK = 128
A_COLS = [H_A * DH_A, DH_A, DH_A, H_IDX * D_IDX, D_IDX, H_IDX]
A_IN = sum(A_COLS)

H_R = 4
DK_R = D_MODEL // H_R
DV_R = 2 * DK_R
RET_THETA = 10000.0
CHUNK = 128
B_COLS = [H_R * DK_R, H_R * DK_R, H_R * DV_R, H_R * DV_R]
B_IN = sum(B_COLS)

D_FF = 4 * D_MODEL

kernel_name = "dsa_retention_interleaved_hybrid"


def _offsets(cols):
    return [int(v) for v in np.cumsum(cols)[:-1]]


def rmsnorm(x, g):
    xf = x.astype(jnp.float32)
    y = xf * lax.rsqrt(jnp.mean(xf * xf, axis=-1, keepdims=True) + RMS_EPS)
    return (y * g.astype(jnp.float32)).astype(x.dtype)


def rotary(x, pos, rot_dim, theta):
    half = rot_dim // 2
    inv = theta ** (-jnp.arange(half, dtype=jnp.float32) / half)
    ang = pos.astype(jnp.float32)[..., None] * inv
    cos = jnp.cos(ang)[:, :, None, :]
    sin = jnp.sin(ang)[:, :, None, :]
    x1 = x[..., :half].astype(jnp.float32)
    x2 = x[..., half:rot_dim].astype(jnp.float32)
    rot = jnp.concatenate([x1 * cos - x2 * sin, x2 * cos + x1 * sin], axis=-1).astype(x.dtype)
    return jnp.concatenate([rot, x[..., rot_dim:]], axis=-1)


def dsa_mixer(h, pos, w_in, w_out):
    B, S, _ = h.shape
    topk = min(TOPK_MAX, S // 4)
    nb = S // Q_BLOCK
    q, k, v, qi, ki, wi = jnp.split(h @ w_in, _offsets(A_COLS), axis=-1)
    q = rotary(q.reshape(B, S, H_A, DH_A), pos, ROT_A, ROPE_THETA)
    k = rotary(k[:, :, None, :], pos, ROT_A, ROPE_THETA)[:, :, 0]
    qi = rotary(qi.reshape(B, S, H_IDX, D_IDX), pos, ROT_IDX, ROPE_THETA)
    ki = rotary(ki[:, :, None, :], pos, ROT_IDX, ROPE_THETA)[:, :, 0]
    wi = wi * (H_IDX ** -0.5 * D_IDX ** -0.5)
    kv = jnp.concatenate([k, v], axis=-1)
    key_idx = jnp.arange(S)
    scale = 1.0 / math.sqrt(DH_A)

    def to_blocks(a):
        return a.reshape((B, nb, Q_BLOCK) + a.shape[2:]).swapaxes(0, 1)

    t_blk = jnp.arange(S).reshape(nb, Q_BLOCK)

    def block(args):
        qb, qib, wb, tb = args
        sc = jnp.einsum('bqhd,bsd->bqhs', qib, ki).astype(jnp.float32)
        idx_score = jnp.einsum('bqhs,bqh->bqs', jax.nn.relu(sc), wb.astype(jnp.float32))
        idx_score = jnp.where(key_idx[None, None, :] <= tb[None, :, None], idx_score, -jnp.inf)
        _, sel = lax.top_k(idx_score, topk)
        kvg = jax.vmap(lambda a, i: a[i])(kv, sel)
        kg, vg = kvg[..., :DH_A], kvg[..., DH_A:]
        valid = sel <= tb[None, :, None]
        logits = jnp.einsum('bqhd,bqkd->bqhk', qb, kg).astype(jnp.float32) * scale
        logits = jnp.where(valid[:, :, None, :], logits, -jnp.inf)
        p = jax.nn.softmax(logits, axis=-1)
        return jnp.einsum('bqhk,bqkd->bqhd', p.astype(vg.dtype), vg)

    o = lax.map(block, (to_blocks(q), to_blocks(qi), to_blocks(wi), t_blk))
    o = o.swapaxes(0, 1).reshape(B, S, H_A * DH_A)
    return o @ w_out


def retention_mixer(h, pos, w_in, gn_g, w_out):
    B, S, _ = h.shape
    n = S // CHUNK
    q, k, v, g = jnp.split(h @ w_in, _offsets(B_COLS), axis=-1)
    q = rotary(q.reshape(B, S, H_R, DK_R), pos, DK_R, RET_THETA)
    k = rotary(k.reshape(B, S, H_R, DK_R), pos, DK_R, RET_THETA) * (DK_R ** -0.5)
    v = v.reshape(B, S, H_R, DV_R)

    def chunks(a):
        return a.reshape(B, n, CHUNK, H_R, a.shape[-1]).transpose(1, 0, 3, 2, 4).astype(jnp.float32)

    log_gamma = jnp.log1p(-jnp.exp2(-5.0 - jnp.arange(H_R, dtype=jnp.float32)))
    j = jnp.arange(CHUNK, dtype=jnp.float32)
    diff = j[:, None] - j[None, :]
    decay = jnp.where(diff >= 0, jnp.exp(jnp.maximum(diff, 0.0) * log_gamma[:, None, None]), 0.0)
    xi = jnp.exp((j + 1.0) * log_gamma[:, None])[..., None]
    zeta = jnp.exp((CHUNK - 1.0 - j) * log_gamma[:, None])[..., None]
    gamma_c = jnp.exp(CHUNK * log_gamma)[:, None, None]

    def step(state, inp):
        qc, kc, vc = inp
        inner = jnp.einsum('bhnd,bhmd->bhnm', qc, kc) * decay
        out = jnp.einsum('bhnm,bhme->bhne', inner, vc) + jnp.einsum('bhnd,bhde->bhne', qc * xi, state)
        state = gamma_c * state + jnp.einsum('bhmd,bhme->bhde', kc * zeta, vc)
        return state, out

    state0 = jnp.zeros((B, H_R, DK_R, DV_R), jnp.float32)
    _, ys = lax.scan(step, state0, (chunks(q), chunks(k), chunks(v)))
    y = ys.transpose(1, 0, 3, 2, 4).reshape(B, S, H_R, DV_R)
    mu = jnp.mean(y, axis=-1, keepdims=True)
    var = jnp.mean(jnp.square(y - mu), axis=-1, keepdims=True)
    y = ((y - mu) * lax.rsqrt(var + RMS_EPS)).reshape(B, S, H_R * DV_R) * gn_g.astype(jnp.float32)
    y = (jax.nn.silu(g.astype(jnp.float32)) * y).astype(h.dtype)
    return y @ w_out


def sqrelu_mlp(h, w_up, w_down):
    return jnp.square(jax.nn.relu(h @ w_up)) @ w_down


def setup_inputs(seed: int = 0) -> dict:
    key = jax.random.key(seed)
    ks = jax.random.split(key, 12)

    def w(k, shape, fan_in):
        return jax.random.normal(k, shape, jnp.float32) * (fan_in ** -0.5)

    def gain(k, shape):
        return 1.0 + 0.02 * jax.random.normal(k, shape, jnp.float32)

    x = jax.random.normal(ks[0], (BATCH, SEQ, D_MODEL), jnp.float32)
    positions = jnp.broadcast_to(jnp.arange(SEQ, dtype=jnp.int32)[None, :], (BATCH, SEQ))
    return {
        "x": x,
        "positions": positions,
        "norm_mix_g": gain(ks[1], (DEPTH, D_MODEL)),
        "norm_mlp_g": gain(ks[2], (DEPTH, D_MODEL)),
        "w_in_a": w(ks[3], (N_A_LAYERS, D_MODEL, A_IN), D_MODEL),
        "w_out_a": w(ks[4], (N_A_LAYERS, H_A * DH_A, D_MODEL), H_A * DH_A),
        "w_in_b": w(ks[5], (N_B_LAYERS, D_MODEL, B_IN), D_MODEL),
        "ret_norm_g": gain(ks[6], (N_B_LAYERS, H_R * DV_R)),
        "w_out_b": w(ks[7], (N_B_LAYERS, H_R * DV_R, D_MODEL), H_R * DV_R),
        "w_mlp_up": w(ks[8], (DEPTH, D_MODEL, D_FF), D_MODEL),
        "w_mlp_down": w(ks[9], (DEPTH, D_FF, D_MODEL), D_FF),
        "final_norm_g": gain(ks[10], (D_MODEL,)),
    }


def reference(x, positions, norm_mix_g, norm_mlp_g, w_in_a, w_out_a, w_in_b, ret_norm_g,
              w_out_b, w_mlp_up, w_mlp_down, final_norm_g):
    h = x
    for i in range(DEPTH):
        j = i // N_MIXERS
        hn = rmsnorm(h, norm_mix_g[i])
        if i % N_MIXERS == 0:
            h = h + dsa_mixer(hn, positions, w_in_a[j], w_out_a[j])
        else:
            h = h + retention_mixer(hn, positions, w_in_b[j], ret_norm_g[j], w_out_b[j])
        h = h + sqrelu_mlp(rmsnorm(h, norm_mlp_g[i]), w_mlp_up[i], w_mlp_down[i])
    return rmsnorm(h, final_norm_g)
```

```python
import functools
import math

import jax
import jax.numpy as jnp
import numpy as np
from jax import lax
from jax.experimental import pallas as pl
from jax.experimental.pallas import tpu as pltpu

D_MODEL = 1024
RMS_EPS = 1e-6
ROPE_THETA = 500000.0
H_A = 8
DH_A = D_MODEL // H_A
ROT_A = DH_A // 4
H_IDX = 8
D_IDX = 64
ROT_IDX = D_IDX // 4
TOPK_MAX = 256
H_R = 4
DK_R = D_MODEL // H_R
DV_R = 2 * DK_R
RET_THETA = 10000.0
D_FF = 4 * D_MODEL

LANES = 128
VMEM_LIMIT_BYTES = 56 * 1024 * 1024

ROW_TILE = 512
FF_TILE = 1024
Q_TILE = 256
K_TILE = 512
RET_CHUNK = 256

NEG_BIG = -1e30
INT_MIN = np.int32(-2 ** 31)
KEY_NEG_INF = np.int32(-2 ** 31 + 0x7FFFFF)
MANT_MASK = np.int32(0x7FFFFFFF)

BF16 = jnp.bfloat16
F32 = jnp.float32


def _dot(a, b):
    return jnp.dot(a, b, preferred_element_type=F32)


def _dot_nt(a, b):
    return lax.dot_general(a, b, (((1,), (1,)), ((), ())), preferred_element_type=F32)


def _dot_tn(a, b):
    return lax.dot_general(a, b, (((0,), (0,)), ((), ())), preferred_element_type=F32)


def _rms(x, g):
    ms = jnp.mean(x * x, axis=-1, keepdims=True)
    return x * lax.rsqrt(ms + RMS_EPS) * g


def _const_spec(shape):
    nd = len(shape)
    return pl.BlockSpec(shape, lambda *_: (0,) * nd, pipeline_mode=pl.Buffered(1))


def _dsa_in_kernel(x_ref, pos_ref, g_ref, wq_ref, wkv_ref, wqi_ref, wki_ref, ww_ref,
                   inv_a_ref, sgn_a_ref, inv_i_ref, sgn_i_ref,
                   q_ref, k_ref, v_ref, qi_ref, kilo_ref, kihi_ref, w_ref):
    hn = _rms(x_ref[...], g_ref[...]).astype(BF16)
    pos = pos_ref[...].astype(F32)
    lane = lax.broadcasted_iota(jnp.int32, (1, LANES), 1)

    ang_a = pos * inv_a_ref[...]
    cos_a = jnp.cos(ang_a)
    sin_a = jnp.sin(ang_a) * sgn_a_ref[...]
    first_a = lane < (ROT_A // 2)

    def rot_a(y):
        sw = jnp.where(first_a, pltpu.roll(y, LANES - ROT_A // 2, 1), pltpu.roll(y, ROT_A // 2, 1))
        return y * cos_a + sw * sin_a

    ang_i = pos * inv_i_ref[...]
    cos_i = jnp.cos(ang_i)
    sin_i = jnp.sin(ang_i) * sgn_i_ref[...]
    first_i = (lane & (D_IDX - 1)) < (ROT_IDX // 2)

    def rot_i(y):
        sw = jnp.where(first_i, pltpu.roll(y, LANES - ROT_IDX // 2, 1), pltpu.roll(y, ROT_IDX // 2, 1))
        return y * cos_i + sw * sin_i

    for c in range(H_A // 2):
        y = _dot(hn, wq_ref[:, c * 2 * DH_A:(c + 1) * 2 * DH_A])
        for t in range(2):
            h = 2 * c + t
            q_ref[:, h * DH_A:(h + 1) * DH_A] = rot_a(y[:, t * DH_A:(t + 1) * DH_A]).astype(BF16)

    kv = _dot(hn, wkv_ref[...])
    k_ref[...] = rot_a(kv[:, :DH_A]).astype(BF16)
    v_ref[...] = kv[:, DH_A:].astype(BF16)

    for c in range(H_IDX * D_IDX // (2 * LANES)):
        y = _dot(hn, wqi_ref[:, c * 2 * LANES:(c + 1) * 2 * LANES])
        for t in range(2):
            j = 2 * c + t
            qi_ref[:, j * LANES:(j + 1) * LANES] = rot_i(y[:, t * LANES:(t + 1) * LANES]).astype(BF16)

    ki = _dot(hn, wki_ref[...])
    kilo_ref[...] = rot_i(ki[:, :LANES]).astype(BF16)
    kihi_ref[...] = rot_i(ki[:, LANES:]).astype(BF16)

    w_ref[...] = _dot(hn, ww_ref[...]) * (H_IDX ** -0.5 * D_IDX ** -0.5)


def _dsa_in_proj(x2, pos2, g, w_in):
    n = x2.shape[0]
    tm = ROW_TILE
    o_q = H_A * DH_A
    o_k = o_q + DH_A
    o_v = o_k + DH_A
    o_qi = o_v + H_IDX * D_IDX
    o_ki = o_qi + D_IDX
    wq = w_in[:, :o_q].astype(BF16)
    wkv = w_in[:, o_q:o_v].astype(BF16)
    wqi = w_in[:, o_v:o_qi].astype(BF16)
    wki_cols = w_in[:, o_qi:o_ki]
    zeros = jnp.zeros_like(wki_cols)
    wki = jnp.concatenate([wki_cols, zeros, zeros, wki_cols], axis=1).astype(BF16)
    ww = jnp.pad(w_in[:, o_ki:], ((0, 0), (0, LANES - H_IDX))).astype(BF16)

    def pattern(rot, period):
        half = rot // 2
        inv = ROPE_THETA ** (-jnp.arange(half, dtype=F32) / half)
        inv_p = jnp.concatenate([inv, inv, jnp.zeros((period - rot,), F32)])
        sgn_p = jnp.concatenate([-jnp.ones((half,), F32), jnp.ones((half,), F32),
                                 jnp.zeros((period - rot,), F32)])
        reps = LANES // period
        return jnp.tile(inv_p, reps)[None, :], jnp.tile(sgn_p, reps)[None, :]

    inv_a, sgn_a = pattern(ROT_A, DH_A)
    inv_i, sgn_i = pattern(ROT_IDX, D_IDX)

    row = lambda width: pl.BlockSpec((tm, width), lambda i: (i, 0))
    outs = [(n, H_A * DH_A, BF16), (n, DH_A, BF16), (n, DH_A, BF16), (n, H_IDX * D_IDX, BF16),
            (n, LANES, BF16), (n, LANES, BF16), (n, LANES, F32)]
    return pl.pallas_call(
        _dsa_in_kernel,
        grid=(n // tm,),
        in_specs=[row(D_MODEL), row(1), _const_spec((1, D_MODEL)),
                  _const_spec(wq.shape), _const_spec(wkv.shape), _const_spec(wqi.shape),
                  _const_spec(wki.shape), _const_spec(ww.shape),
                  _const_spec((1, LANES)), _const_spec((1, LANES)),
                  _const_spec((1, LANES)), _const_spec((1, LANES))],
        out_specs=[row(s[1]) for s in outs],
        out_shape=[jax.ShapeDtypeStruct(s[:2], s[2]) for s in outs],
        compiler_params=pltpu.CompilerParams(dimension_semantics=("parallel",),
                                             vmem_limit_bytes=VMEM_LIMIT_BYTES),
        name="dsa_in_proj",
    )(x2, pos2, g[None, :], wq, wkv, wqi, wki, ww, inv_a, sgn_a, inv_i, sgn_i)


def _key_to_float(key):
    key = jnp.maximum(key, KEY_NEG_INF)
    bits = jnp.where(key >= 0, key, key ^ MANT_MASK)
    return lax.bitcast_convert_type(bits, F32)


def _dsa_attn_kernel(q_ref, qi_ref, w_ref, k_ref, v_ref, kilo_ref, kihi_ref, o_ref,
                     sc_ref, m_ref, l_ref, acc_ref, cut_ref, *, topk):
    tq = q_ref.shape[0]
    s_len = k_ref.shape[0]
    q0 = pl.program_id(1) * tq
    n_ck = lax.div(q0 + (tq + K_TILE - 1), jnp.int32(K_TILE))
    row_t = q0 + lax.broadcasted_iota(jnp.int32, (tq, 1), 0)
    lane_k = lax.broadcasted_iota(jnp.int32, (1, K_TILE), 1)
    w = w_ref[...]

    def score_body(j, carry):
        off = pl.multiple_of(j * K_TILE, K_TILE)
        klo = kilo_ref[pl.ds(off, K_TILE), :]
        khi = kihi_ref[pl.ds(off, K_TILE), :]
        acc = jnp.zeros((tq, K_TILE), F32)
        for p in range(H_IDX // 2):
            a = qi_ref[:, p * LANES:(p + 1) * LANES]
            acc += jnp.maximum(_dot_nt(a, klo), 0.0) * w[:, 2 * p:2 * p + 1]
            acc += jnp.maximum(_dot_nt(a, khi), 0.0) * w[:, 2 * p + 1:2 * p + 2]
        sc_ref[:, pl.ds(off, K_TILE)] = jnp.where(off + lane_k <= row_t, acc, -jnp.inf)
        return carry

    lax.fori_loop(0, n_ck, score_body, 0)

    def count_ge(thr):
        thr_b = jnp.broadcast_to(thr, (tq, LANES))

        def body(j, part):
            off = pl.multiple_of(j * K_TILE, K_TILE)
            sc = sc_ref[:, pl.ds(off, K_TILE)]
            for t in range(K_TILE // LANES):
                part += jnp.where(sc[:, t * LANES:(t + 1) * LANES] >= thr_b, 1.0, 0.0)
            return part

        part = lax.fori_loop(0, n_ck, body, jnp.zeros((tq, LANES), F32))
        return jnp.sum(part, axis=-1, keepdims=True)

    def search_body(it, carry):
        ukey, cnt_in, cnt_out = carry
        cand = ukey | lax.shift_left(jnp.int32(1), 31 - it)
        cnt = count_ge(_key_to_float(cand ^ INT_MIN))
        ok = cnt >= float(topk)
        return (jnp.where(ok, cand, ukey), jnp.where(ok, cnt, cnt_in), jnp.where(ok, cnt_out, cnt))

    zeros = jnp.zeros((tq, 1), F32)
    ukey, cnt_ge, cnt_gt = lax.fori_loop(
        0, 32, search_body, (jnp.zeros((tq, 1), jnp.int32), zeros + float(s_len), zeros))
    key = ukey ^ INT_MIN
    thr = _key_to_float(key)
    thr_up = _key_to_float(key + 1)
    need = float(topk) - cnt_gt

    cut_ref[...] = jnp.full(cut_ref.shape, s_len, jnp.int32)
    excess = jnp.where((cnt_ge - cnt_gt > need) & (key > KEY_NEG_INF), 1, 0)

    @pl.when(jnp.max(excess) > 0)
    def _():
        def count_ties_before(c):
            def body(j, part):
                off = pl.multiple_of(j * K_TILE, K_TILE)
                sc = sc_ref[:, pl.ds(off, K_TILE)]
                tie = (sc >= thr) & jnp.logical_not(sc >= thr_up) & (off + lane_k < c)
                tie = jnp.where(tie, 1.0, 0.0)
                for t in range(K_TILE // LANES):
                    part += tie[:, t * LANES:(t + 1) * LANES]
                return part

            part = lax.fori_loop(0, n_ck, body, jnp.zeros((tq, LANES), F32))
            return jnp.sum(part, axis=-1, keepdims=True)

        def cut_body(it, c):
            cand = c | lax.shift_left(jnp.int32(1), (s_len.bit_length() - 1) - it)
            return jnp.where(count_ties_before(cand) <= need - 1.0, cand, c)

        c = lax.fori_loop(0, s_len.bit_length(), cut_body, jnp.zeros((tq, 1), jnp.int32))
        cut_ref[...] = jnp.broadcast_to(c, cut_ref.shape)

    cut = cut_ref[:, 0:1]

    m_ref[...] = jnp.full(m_ref.shape, NEG_BIG, F32)
    l_ref[...] = jnp.zeros(l_ref.shape, F32)
    acc_ref[...] = jnp.zeros(acc_ref.shape, F32)
    scale = 1.0 / math.sqrt(DH_A)

    def attn_body(j, carry):
        off = pl.multiple_of(j * K_TILE, K_TILE)
        sc = sc_ref[:, pl.ds(off, K_TILE)]
        idx = off + lane_k
        sel = (sc >= thr_up) | ((sc >= thr) & (idx <= cut))
        bias = jnp.where(sel & (idx <= row_t), 0.0, NEG_BIG)
        kc = k_ref[pl.ds(off, K_TILE), :]
        vc = v_ref[pl.ds(off, K_TILE), :]
        for h in range(H_A):
            lg = _dot_nt(q_ref[:, h * DH_A:(h + 1) * DH_A], kc) * scale + bias
            m_prev = m_ref[h]
            m_new = jnp.maximum(m_prev, jnp.max(lg, axis=-1, keepdims=True))
            alpha = jnp.exp(m_prev - m_new)
            p = jnp.exp(lg - m_new[:, 0:1])
            l_ref[h] = alpha * l_ref[h] + jnp.sum(p, axis=-1, keepdims=True)
            acc_ref[h] = alpha * acc_ref[h] + _dot(p.astype(BF16), vc)
            m_ref[h] = m_new
        return carry

    lax.fori_loop(0, n_ck, attn_body, 0)

    for h in range(H_A):
        o_ref[:, h * DH_A:(h + 1) * DH_A] = (acc_ref[h] / l_ref[h]).astype(BF16)


def _dsa_attention(q, qi, w, k, v, kilo, kihi):
    b, s, _ = q.shape
    tq = Q_TILE
    topk = min(TOPK_MAX, s // 4)
    qt = lambda width: pl.BlockSpec((None, tq, width), lambda bi, qi_: (bi, qi_, 0))
    full = lambda width: pl.BlockSpec((None, s, width), lambda bi, qi_: (bi, 0, 0))
    return pl.pallas_call(
        functools.partial(_dsa_attn_kernel, topk=topk),
        grid=(b, s // tq),
        in_specs=[qt(H_A * DH_A), qt(H_IDX * D_IDX), qt(LANES),
                  full(DH_A), full(DH_A), full(LANES), full(LANES)],
        out_specs=qt(H_A * DH_A),
        out_shape=jax.ShapeDtypeStruct((b, s, H_A * DH_A), BF16),
        scratch_shapes=[pltpu.VMEM((tq, s), F32),
                        pltpu.VMEM((H_A, tq, LANES), F32),
                        pltpu.VMEM((H_A, tq, LANES), F32),
                        pltpu.VMEM((H_A, tq, DH_A), F32),
                        pltpu.VMEM((tq, LANES), jnp.int32)],
        compiler_params=pltpu.CompilerParams(dimension_semantics=("parallel", "parallel"),
                                             vmem_limit_bytes=VMEM_LIMIT_BYTES),
        name="dsa_attention",
    )(q, qi, w, k, v, kilo, kihi)


def _out_mlp_kernel(x_ref, y_ref, wo_ref, g_ref, wup_ref, wdn_ref, gf_ref, o_ref, *, final_norm):
    h = x_ref[...] + _dot(y_ref[...], wo_ref[...])
    hn = _rms(h, g_ref[...]).astype(BF16)
    acc = jnp.zeros(h.shape, F32)
    for c in range(D_FF // FF_TILE):
        u = jnp.maximum(_dot(hn, wup_ref[:, c * FF_TILE:(c + 1) * FF_TILE]), 0.0)
        acc += _dot((u * u).astype(BF16), wdn_ref[c * FF_TILE:(c + 1) * FF_TILE, :])
    h = h + acc
    if final_norm:
        h = _rms(h, gf_ref[...])
    o_ref[...] = h


def _out_mlp(x2, y2, w_out, g_mlp, w_up, w_down, g_final, final_norm):
    n = x2.shape[0]
    tm = ROW_TILE
    kin = y2.shape[1]
    row = lambda width: pl.BlockSpec((tm, width), lambda i: (i, 0))
    return pl.pallas_call(
        functools.partial(_out_mlp_kernel, final_norm=final_norm),
        grid=(n // tm,),
        in_specs=[row(D_MODEL), row(kin), _const_spec((kin, D_MODEL)), _const_spec((1, D_MODEL)),
                  _const_spec((D_MODEL, D_FF)), _const_spec((D_FF, D_MODEL)),
                  _const_spec((1, D_MODEL))],
        out_specs=row(D_MODEL),
        out_shape=jax.ShapeDtypeStruct((n, D_MODEL), F32),
        compiler_params=pltpu.CompilerParams(dimension_semantics=("parallel",),
                                             vmem_limit_bytes=VMEM_LIMIT_BYTES),
        name="out_proj_mlp_final" if final_norm else "out_proj_mlp",
    )(x2, y2, w_out.astype(BF16), g_mlp[None, :], w_up.astype(BF16), w_down.astype(BF16),
      g_final[None, :])


def _ret_in_kernel(x_ref, pos_ref, g_ref, wq_ref, wk_ref, wv_ref, wg_ref, inv_ref,
                   q_ref, k_ref, v_ref, gate_ref):
    hn = _rms(x_ref[...], g_ref[...]).astype(BF16)
    ang = pos_ref[...].astype(F32) * inv_ref[...]
    cos = jnp.cos(ang)
    sin = jnp.sin(ang)
    half = DK_R // 2

    def rot_store(w_ref, out_ref, mul):
        for h in range(H_R):
            y = _dot(hn, w_ref[:, h * DK_R:(h + 1) * DK_R])
            x1 = y[:, :half]
            x2 = y[:, half:]
            out_ref[:, h * DK_R:h * DK_R + half] = ((x1 * cos - x2 * sin) * mul).astype(BF16)
            out_ref[:, h * DK_R + half:(h + 1) * DK_R] = ((x2 * cos + x1 * sin) * mul).astype(BF16)

    rot_store(wq_ref, q_ref, 1.0)
    rot_store(wk_ref, k_ref, DK_R ** -0.5)
    for c in range(H_R):
        v_ref[:, c * DV_R:(c + 1) * DV_R] = _dot(hn, wv_ref[:, c * DV_R:(c + 1) * DV_R]).astype(BF16)
        gate_ref[:, c * DV_R:(c + 1) * DV_R] = _dot(hn, wg_ref[:, c * DV_R:(c + 1) * DV_R]).astype(BF16)


def _ret_in_proj(x2, pos2, g, w_in):
    n = x2.shape[0]
    tm = ROW_TILE
    dq = H_R * DK_R
    dv = H_R * DV_R
    wq = w_in[:, :dq].astype(BF16)
    wk = w_in[:, dq:2 * dq].astype(BF16)
    wv = w_in[:, 2 * dq:2 * dq + dv].astype(BF16)
    wg = w_in[:, 2 * dq + dv:].astype(BF16)
    half = DK_R // 2
    inv = (RET_THETA ** (-jnp.arange(half, dtype=F32) / half))[None, :]
    row = lambda width: pl.BlockSpec((tm, width), lambda i: (i, 0))
    outs = [(n, dq), (n, dq), (n, dv), (n, dv)]
    return pl.pallas_call(
        _ret_in_kernel,
        grid=(n // tm,),
        in_specs=[row(D_MODEL), row(1), _const_spec((1, D_MODEL)),
                  _const_spec(wq.shape), _const_spec(wk.shape), _const_spec(wv.shape),
                  _const_spec(wg.shape), _const_spec((1, half))],
        out_specs=[row(s[1]) for s in outs],
        out_shape=[jax.ShapeDtypeStruct(s, BF16) for s in outs],
        compiler_params=pltpu.CompilerParams(dimension_semantics=("parallel",),
                                             vmem_limit_bytes=VMEM_LIMIT_BYTES),
        name="ret_in_proj",
    )(x2, pos2, g[None, :], wq, wk, wv, wg, inv)


def _retention_kernel(lg_ref, q_ref, k_ref, v_ref, gate_ref, gn_ref, y_ref, state_ref):
    c = q_ref.shape[0]

    @pl.when(pl.program_id(2) == 0)
    def _():
        state_ref[...] = jnp.zeros(state_ref.shape, F32)

    lg = lg_ref[0:1, 0:1]
    i_col = lax.broadcasted_iota(jnp.int32, (c, 1), 0).astype(F32)
    j_row = lax.broadcasted_iota(jnp.int32, (1, c), 1).astype(F32)
    diff = i_col - j_row
    decay = jnp.where(diff >= 0.0, jnp.exp(jnp.maximum(diff, 0.0) * lg), 0.0)
    xi = jnp.exp((i_col + 1.0) * lg)
    zeta = jnp.exp((float(c) - 1.0 - i_col) * lg)
    gamma_c = jnp.exp(float(c) * lg)

    q = q_ref[...]
    k = k_ref[...]
    v = v_ref[...]
    state = state_ref[...]
    inner = _dot_nt(q, k) * decay
    out = _dot(inner.astype(BF16), v) + _dot((q.astype(F32) * xi).astype(BF16), state.astype(BF16))
    state_ref[...] = gamma_c * state + _dot_tn((k.astype(F32) * zeta).astype(BF16), v)

    mu = jnp.mean(out, axis=-1, keepdims=True)
    d = out - mu
    var = jnp.mean(d * d, axis=-1, keepdims=True)
    yn = d * lax.rsqrt(var + RMS_EPS) * gn_ref[...]
    gate = gate_ref[...].astype(F32)
    y_ref[...] = (gate * jax.nn.sigmoid(gate) * yn).astype(BF16)


def _retention(q, k, v, gate, gn_g):
    b, s, _ = q.shape
    c = RET_CHUNK
    log_gamma = jnp.log1p(-jnp.exp2(-5.0 - jnp.arange(H_R, dtype=F32)))
    lg_tab = jnp.broadcast_to(log_gamma[:, None, None], (H_R, 8, LANES))
    qk_spec = pl.BlockSpec((None, c, DK_R), lambda bi, h, ci: (bi, ci, h))
    v_spec = pl.BlockSpec((None, c, DV_R), lambda bi, h, ci: (bi, ci, h))
    return pl.pallas_call(
        _retention_kernel,
        grid=(b, H_R, s // c),
        in_specs=[pl.BlockSpec((None, 8, LANES), lambda bi, h, ci: (h, 0, 0)),
                  qk_spec, qk_spec, v_spec, v_spec,
                  pl.BlockSpec((1, DV_R), lambda bi, h, ci: (0, h))],
        out_specs=v_spec,
        out_shape=jax.ShapeDtypeStruct((b, s, H_R * DV_R), BF16),
        scratch_shapes=[pltpu.VMEM((DK_R, DV_R), F32)],
        compiler_params=pltpu.CompilerParams(
            dimension_semantics=("parallel", "parallel", "arbitrary"),
            vmem_limit_bytes=VMEM_LIMIT_BYTES),
        name="retention",
    )(lg_tab, q, k, v, gate, gn_g[None, :])


def kernel(x, positions, norm_mix_g, norm_mlp_g, w_in_a, w_out_a, w_in_b, ret_norm_g, w_out_b,
           w_mlp_up, w_mlp_down, final_norm_g):
    b, s, d = x.shape
    n = b * s
    x2 = x.reshape(n, d)
    pos2 = positions.reshape(n, 1)

    q, k, v, qi, kilo, kihi, w = _dsa_in_proj(x2, pos2, norm_mix_g[0], w_in_a[0])
    r3 = lambda a: a.reshape(b, s, a.shape[-1])
    o = _dsa_attention(r3(q), r3(qi), r3(w), r3(k), r3(v), r3(kilo), r3(kihi))
    h = _out_mlp(x2, o.reshape(n, -1), w_out_a[0], norm_mlp_g[0], w_mlp_up[0], w_mlp_down[0],
                 final_norm_g, final_norm=False)

    rq, rk, rv, rg = _ret_in_proj(h, pos2, norm_mix_g[1], w_in_b[0])
    y = _retention(r3(rq), r3(rk), r3(rv), r3(rg), ret_norm_g[0])
    out = _out_mlp(h, y.reshape(n, -1), w_out_b[0], norm_mlp_g[1], w_mlp_up[1], w_mlp_down[1],
                   final_norm_g, final_norm=True)
    return out.reshape(b, s, d)
```

```python
import functools
import math

import jax
import jax.numpy as jnp
import numpy as np
from jax import lax
from jax.experimental import pallas as pl
from jax.experimental.pallas import tpu as pltpu

D_MODEL = 1024
RMS_EPS = 1e-6
ROPE_THETA = 500000.0
H_A = 8
DH_A = D_MODEL // H_A
ROT_A = DH_A // 4
H_IDX = 8
D_IDX = 64
ROT_IDX = D_IDX // 4
TOPK_MAX = 256
H_R = 4
DK_R = D_MODEL // H_R
DV_R = 2 * DK_R
RET_THETA = 10000.0
D_FF = 4 * D_MODEL

LANES = 128
VMEM_LIMIT_BYTES = 56 * 1024 * 1024

ROW_TILE = 512
FF_TILE = 1024
Q_TILE = 256
K_TILE = 512
RET_CHUNK = 256

NEG_BIG = -1e30
INT_MIN = np.int32(-2 ** 31)
KEY_NEG_INF = np.int32(-2 ** 31 + 0x7FFFFF)
MANT_MASK = np.int32(0x7FFFFFFF)

BF16 = jnp.bfloat16
F32 = jnp.float32


def _dot(a, b):
    return jnp.dot(a, b, preferred_element_type=F32)


def _dot_nt(a, b):
    return lax.dot_general(a, b, (((1,), (1,)), ((), ())), preferred_element_type=F32)


def _dot_tn(a, b):
    return lax.dot_general(a, b, (((0,), (0,)), ((), ())), preferred_element_type=F32)


def _rms(x, g):
    ms = jnp.mean(x * x, axis=-1, keepdims=True)
    return x * lax.rsqrt(ms + RMS_EPS) * g


def _const_spec(shape):
    nd = len(shape)
    return pl.BlockSpec(shape, lambda *_: (0,) * nd, pipeline_mode=pl.Buffered(1))


def _dsa_in_kernel(x_ref, pos_ref, g_ref, wq_ref, wkv_ref, wqi_ref, wki_ref, ww_ref,
                   inv_a_ref, sgn_a_ref, inv_i_ref, sgn_i_ref,
                   q_ref, k_ref, vt_ref, qi_ref, kilo_ref, kihi_ref, w_ref):
    hn = _rms(x_ref[...], g_ref[...]).astype(BF16)
    pos = pos_ref[...].astype(F32)
    lane = lax.broadcasted_iota(jnp.int32, (1, LANES), 1)

    ang_a = pos * inv_a_ref[...]
    cos_a = jnp.cos(ang_a)
    sin_a = jnp.sin(ang_a) * sgn_a_ref[...]
    first_a = lane < (ROT_A // 2)

    def rot_a(y):
        sw = jnp.where(first_a, pltpu.roll(y, LANES - ROT_A // 2, 1), pltpu.roll(y, ROT_A // 2, 1))
        return y * cos_a + sw * sin_a

    ang_i = pos * inv_i_ref[...]
    cos_i = jnp.cos(ang_i)
    sin_i = jnp.sin(ang_i) * sgn_i_ref[...]
    first_i = (lane & (D_IDX - 1)) < (ROT_IDX // 2)

    def rot_i(y):
        sw = jnp.where(first_i, pltpu.roll(y, LANES - ROT_IDX // 2, 1), pltpu.roll(y, ROT_IDX // 2, 1))
        return y * cos_i + sw * sin_i

    for c in range(H_A // 2):
        y = _dot(hn, wq_ref[:, c * 2 * DH_A:(c + 1) * 2 * DH_A])
        for t in range(2):
            h = 2 * c + t
            q_ref[2 * c + t] = rot_a(y[:, t * DH_A:(t + 1) * DH_A]).astype(BF16)

    kv = _dot(hn, wkv_ref[...])
    k_ref[...] = rot_a(kv[:, :DH_A]).astype(BF16)
    vt_ref[...] = kv[:, DH_A:].T.astype(BF16)

    for c in range(H_IDX * D_IDX // (2 * LANES)):
        y = _dot(hn, wqi_ref[:, c * 2 * LANES:(c + 1) * 2 * LANES])
        for t in range(2):
            j = 2 * c + t
            qi_ref[:, j * LANES:(j + 1) * LANES] = rot_i(y[:, t * LANES:(t + 1) * LANES]).astype(BF16)

    ki = _dot(hn, wki_ref[...])
    kilo_ref[...] = rot_i(ki[:, :LANES]).astype(BF16)
    kihi_ref[...] = rot_i(ki[:, LANES:]).astype(BF16)

    w_ref[...] = _dot(hn, ww_ref[...]) * (H_IDX ** -0.5 * D_IDX ** -0.5)


def _dsa_in_proj(x2, pos2, g, w_in, b, s):
    n = x2.shape[0]
    tm = ROW_TILE
    tiles_per_seq = s // tm
    o_q = H_A * DH_A
    o_k = o_q + DH_A
    o_v = o_k + DH_A
    o_qi = o_v + H_IDX * D_IDX
    o_ki = o_qi + D_IDX
    wq = w_in[:, :o_q].astype(BF16)
    wkv = w_in[:, o_q:o_v].astype(BF16)
    wqi = w_in[:, o_v:o_qi].astype(BF16)
    wki_cols = w_in[:, o_qi:o_ki]
    zeros = jnp.zeros_like(wki_cols)
    wki = jnp.concatenate([wki_cols, zeros, zeros, wki_cols], axis=1).astype(BF16)
    ww = jnp.pad(w_in[:, o_ki:], ((0, 0), (0, LANES - H_IDX))).astype(BF16)

    def pattern(rot, period):
        half = rot // 2
        inv = ROPE_THETA ** (-jnp.arange(half, dtype=F32) / half)
        inv_p = jnp.concatenate([inv, inv, jnp.zeros((period - rot,), F32)])
        sgn_p = jnp.concatenate([-jnp.ones((half,), F32), jnp.ones((half,), F32),
                                 jnp.zeros((period - rot,), F32)])
        reps = LANES // period
        return jnp.tile(inv_p, reps)[None, :], jnp.tile(sgn_p, reps)[None, :]

    inv_a, sgn_a = pattern(ROT_A, DH_A)
    inv_i, sgn_i = pattern(ROT_IDX, D_IDX)

    row = lambda width: pl.BlockSpec((tm, width), lambda i: (i, 0))
    rows_out = lambda width, dt: (row(width), jax.ShapeDtypeStruct((n, width), dt))
    vt_out = (pl.BlockSpec((None, DH_A, tm), lambda i: (i // tiles_per_seq, 0, i % tiles_per_seq)),
              jax.ShapeDtypeStruct((b, DH_A, s), BF16))
    q_out = (pl.BlockSpec((None, H_A, tm, DH_A),
                          lambda i: (i // tiles_per_seq, 0, i % tiles_per_seq, 0)),
             jax.ShapeDtypeStruct((b, H_A, s, DH_A), BF16))
    outs = [q_out, rows_out(DH_A, BF16), vt_out, rows_out(H_IDX * D_IDX, BF16),
            rows_out(LANES, BF16), rows_out(LANES, BF16), rows_out(LANES, F32)]
    return pl.pallas_call(
        _dsa_in_kernel,
        grid=(n // tm,),
        in_specs=[row(D_MODEL), row(1), _const_spec((1, D_MODEL)),
                  _const_spec(wq.shape), _const_spec(wkv.shape), _const_spec(wqi.shape),
                  _const_spec(wki.shape), _const_spec(ww.shape),
                  _const_spec((1, LANES)), _const_spec((1, LANES)),
                  _const_spec((1, LANES)), _const_spec((1, LANES))],
        out_specs=[o[0] for o in outs],
        out_shape=[o[1] for o in outs],
        compiler_params=pltpu.CompilerParams(dimension_semantics=("parallel",),
                                             vmem_limit_bytes=VMEM_LIMIT_BYTES),
        name="dsa_in_proj",
    )(x2, pos2, g[None, :], wq, wkv, wqi, wki, ww, inv_a, sgn_a, inv_i, sgn_i)


def _key_to_float(key):
    key = jnp.maximum(key, KEY_NEG_INF)
    bits = jnp.where(key >= 0, key, key ^ MANT_MASK)
    return lax.bitcast_convert_type(bits, F32)


def _sum_row_groups(x, n_acc=4):
    groups = [x[g * 8:(g + 1) * 8, :] for g in range(x.shape[0] // 8)]
    accs = groups[:n_acc]
    for i, g in enumerate(groups[n_acc:]):
        accs[i % n_acc] = accs[i % n_acc] + g
    while len(accs) > 1:
        accs = [a + b for a, b in zip(accs[::2], accs[1::2])]
    return accs[0]


def _dsa_attn_kernel(q_ref, qi_ref, w_ref, k_ref, vt_ref, kilo_ref, kihi_ref, o_ref,
                     sc_ref, m_ref, l_ref, acc_ref, cut_ref, *, topk):
    tq = qi_ref.shape[0]
    s_len = k_ref.shape[0]
    q0 = pl.program_id(1) * tq
    n_ck = lax.div(q0 + (tq + K_TILE - 1), jnp.int32(K_TILE))
    query_t = q0 + lax.broadcasted_iota(jnp.int32, (1, tq), 1)
    key_i = lax.broadcasted_iota(jnp.int32, (K_TILE, 1), 0)
    w_t = w_ref[...].T

    def score_body(j, carry):
        off = pl.multiple_of(j * K_TILE, K_TILE)
        klo = kilo_ref[pl.ds(off, K_TILE), :]
        khi = kihi_ref[pl.ds(off, K_TILE), :]
        acc = jnp.zeros((K_TILE, tq), F32)
        for p in range(H_IDX // 2):
            a = qi_ref[:, p * LANES:(p + 1) * LANES]
            acc += jnp.maximum(_dot_nt(klo, a), 0.0) * w_t[2 * p:2 * p + 1, :]
            acc += jnp.maximum(_dot_nt(khi, a), 0.0) * w_t[2 * p + 1:2 * p + 2, :]
        sc_ref[pl.ds(off, K_TILE), :] = jnp.where(off + key_i <= query_t, acc, -jnp.inf)
        return carry

    lax.fori_loop(0, n_ck, score_body, 0)

    def count_ge(thr):
        def body(j, part):
            off = pl.multiple_of(j * K_TILE, K_TILE)
            sc = sc_ref[pl.ds(off, K_TILE), :]
            return part + _sum_row_groups(jnp.where(sc >= thr, 1.0, 0.0))

        part = lax.fori_loop(0, n_ck, body, jnp.zeros((8, tq), F32))
        return jnp.sum(part, axis=0, keepdims=True)

    def search_cond(carry):
        it, _, _, _, settled = carry
        return (it < 32) & (jnp.min(settled) == 0)

    def search_body(carry):
        it, ukey, cnt_in, cnt_out, settled = carry
        cand = ukey | lax.shift_left(jnp.int32(1), 31 - it)
        cnt = count_ge(_key_to_float(cand ^ INT_MIN))
        live = settled == 0
        ok = live & (cnt >= float(topk))
        bad = live & (cnt < float(topk))
        return (it + 1, jnp.where(ok, cand, ukey), jnp.where(ok, cnt, cnt_in),
                jnp.where(bad, cnt, cnt_out), jnp.where(ok & (cnt == float(topk)), 1, settled))

    zeros = jnp.zeros((1, tq), F32)
    _, ukey, cnt_ge, cnt_gt, settled = lax.while_loop(
        search_cond, search_body,
        (jnp.int32(0), jnp.zeros((1, tq), jnp.int32), zeros + float(s_len), zeros,
         jnp.zeros((1, tq), jnp.int32)))
    key = ukey ^ INT_MIN
    thr = _key_to_float(key)
    thr_up = jnp.where(settled == 1, thr, _key_to_float(key + 1))
    need = float(topk) - cnt_gt

    cut_ref[...] = jnp.full(cut_ref.shape, s_len, jnp.int32)
    excess = jnp.where((settled == 0) & (cnt_ge - cnt_gt > need) & (key > KEY_NEG_INF), 1, 0)

    @pl.when(jnp.max(excess) > 0)
    def _():
        def count_ties_before(c):
            def body(j, part):
                off = pl.multiple_of(j * K_TILE, K_TILE)
                sc = sc_ref[pl.ds(off, K_TILE), :]
                tie = (sc >= thr) & jnp.logical_not(sc >= thr_up) & (off + key_i < c)
                return part + _sum_row_groups(jnp.where(tie, 1.0, 0.0))

            part = lax.fori_loop(0, n_ck, body, jnp.zeros((8, tq), F32))
            return jnp.sum(part, axis=0, keepdims=True)

        def cut_body(it, c):
            cand = c | lax.shift_left(jnp.int32(1), (s_len.bit_length() - 1) - it)
            return jnp.where(count_ties_before(cand) <= need - 1.0, cand, c)

        c = lax.fori_loop(0, s_len.bit_length(), cut_body, jnp.zeros((1, tq), jnp.int32))
        cut_ref[...] = jnp.broadcast_to(c, cut_ref.shape)

    cut = cut_ref[0:1, :]

    m_ref[...] = jnp.full(m_ref.shape, NEG_BIG, F32)
    l_ref[...] = jnp.zeros(l_ref.shape, F32)
    acc_ref[...] = jnp.zeros(acc_ref.shape, F32)
    scale = math.log2(math.e) / math.sqrt(DH_A)

    def attn_body(j, carry):
        off = pl.multiple_of(j * K_TILE, K_TILE)
        sc = sc_ref[pl.ds(off, K_TILE), :]
        idx = off + key_i
        sel = (sc >= thr_up) | ((sc >= thr) & (idx <= cut))
        bias = jnp.where(sel & (idx <= query_t), 0.0, NEG_BIG)
        kc = k_ref[pl.ds(off, K_TILE), :]
        vtc = vt_ref[:, pl.ds(off, K_TILE)]
        q_all = q_ref[...].reshape(H_A * tq, DH_A)
        lg = _dot_nt(kc, q_all) * scale + jnp.tile(bias, (1, H_A))
        m_prev = m_ref[...]
        m_new = jnp.maximum(m_prev, jnp.max(lg, axis=0, keepdims=True))
        alpha = jnp.exp2(m_prev - m_new)
        p = jnp.exp2(lg - m_new)
        l_ref[...] = alpha * l_ref[...] + jnp.sum(p, axis=0, keepdims=True)
        acc_ref[...] = alpha * acc_ref[...] + _dot(vtc, p.astype(BF16))
        m_ref[...] = m_new
        return carry

    lax.fori_loop(0, n_ck, attn_body, 0)

    out_t = acc_ref[...] / l_ref[...]
    for h in range(H_A):
        o_ref[:, h * DH_A:(h + 1) * DH_A] = out_t[:, h * tq:(h + 1) * tq].T.astype(BF16)


def _dsa_attention(q, qi, w, k, vt, kilo, kihi):
    b, s, _ = k.shape
    tq = Q_TILE
    topk = min(TOPK_MAX, s // 4)
    qt = lambda width: pl.BlockSpec((None, tq, width), lambda bi, qi_: (bi, qi_, 0))
    full = lambda width: pl.BlockSpec((None, s, width), lambda bi, qi_: (bi, 0, 0))
    return pl.pallas_call(
        functools.partial(_dsa_attn_kernel, topk=topk),
        grid=(b, s // tq),
        in_specs=[pl.BlockSpec((None, H_A, tq, DH_A), lambda bi, qi_: (bi, 0, qi_, 0)),
                  qt(H_IDX * D_IDX), qt(LANES), full(DH_A),
                  pl.BlockSpec((None, DH_A, s), lambda bi, qi_: (bi, 0, 0)),
                  full(LANES), full(LANES)],
        out_specs=qt(H_A * DH_A),
        out_shape=jax.ShapeDtypeStruct((b, s, H_A * DH_A), BF16),
        scratch_shapes=[pltpu.VMEM((s, tq), F32),
                        pltpu.VMEM((1, H_A * tq), F32),
                        pltpu.VMEM((1, H_A * tq), F32),
                        pltpu.VMEM((DH_A, H_A * tq), F32),
                        pltpu.VMEM((8, tq), jnp.int32)],
        compiler_params=pltpu.CompilerParams(dimension_semantics=("parallel", "parallel"),
                                             vmem_limit_bytes=VMEM_LIMIT_BYTES),
        name="dsa_attention",
    )(q, qi, w, k, vt, kilo, kihi)


def _out_mlp_kernel(x_ref, y_ref, wo_ref, g_ref, wup_ref, wdn_ref, gf_ref, o_ref, *, final_norm):
    h = x_ref[...] + _dot(y_ref[...], wo_ref[...])
    hn = _rms(h, g_ref[...]).astype(BF16)
    acc = jnp.zeros(h.shape, F32)
    for c in range(D_FF // FF_TILE):
        u = jnp.maximum(_dot(hn, wup_ref[:, c * FF_TILE:(c + 1) * FF_TILE]), 0.0)
        acc += _dot((u * u).astype(BF16), wdn_ref[c * FF_TILE:(c + 1) * FF_TILE, :])
    h = h + acc
    if final_norm:
        h = _rms(h, gf_ref[...])
    o_ref[...] = h


def _out_mlp(x2, y2, w_out, g_mlp, w_up, w_down, g_final, final_norm):
    n = x2.shape[0]
    tm = ROW_TILE
    kin = y2.shape[1]
    row = lambda width: pl.BlockSpec((tm, width), lambda i: (i, 0))
    return pl.pallas_call(
        functools.partial(_out_mlp_kernel, final_norm=final_norm),
        grid=(n // tm,),
        in_specs=[row(D_MODEL), row(kin), _const_spec((kin, D_MODEL)), _const_spec((1, D_MODEL)),
                  _const_spec((D_MODEL, D_FF)), _const_spec((D_FF, D_MODEL)),
                  _const_spec((1, D_MODEL))],
        out_specs=row(D_MODEL),
        out_shape=jax.ShapeDtypeStruct((n, D_MODEL), F32),
        compiler_params=pltpu.CompilerParams(dimension_semantics=("parallel",),
                                             vmem_limit_bytes=VMEM_LIMIT_BYTES),
        name="out_proj_mlp_final" if final_norm else "out_proj_mlp",
    )(x2, y2, w_out.astype(BF16), g_mlp[None, :], w_up.astype(BF16), w_down.astype(BF16),
      g_final[None, :])


def _ret_in_kernel(x_ref, pos_ref, g_ref, wq_ref, wk_ref, wv_ref, wg_ref, inv_ref,
                   q_ref, k_ref, v_ref, gate_ref):
    hn = _rms(x_ref[...], g_ref[...]).astype(BF16)
    ang = pos_ref[...].astype(F32) * inv_ref[...]
    cos = jnp.cos(ang)
    sin = jnp.sin(ang)
    half = DK_R // 2

    def rot_store(w_ref, out_ref, mul):
        for h in range(H_R):
            y = _dot(hn, w_ref[:, h * DK_R:(h + 1) * DK_R])
            x1 = y[:, :half]
            x2 = y[:, half:]
            out_ref[:, h * DK_R:h * DK_R + half] = ((x1 * cos - x2 * sin) * mul).astype(BF16)
            out_ref[:, h * DK_R + half:(h + 1) * DK_R] = ((x2 * cos + x1 * sin) * mul).astype(BF16)

    rot_store(wq_ref, q_ref, 1.0)
    rot_store(wk_ref, k_ref, DK_R ** -0.5)
    for c in range(H_R):
        v_ref[:, c * DV_R:(c + 1) * DV_R] = _dot(hn, wv_ref[:, c * DV_R:(c + 1) * DV_R]).astype(BF16)
        gate_ref[:, c * DV_R:(c + 1) * DV_R] = _dot(hn, wg_ref[:, c * DV_R:(c + 1) * DV_R]).astype(BF16)


def _ret_in_proj(x2, pos2, g, w_in):
    n = x2.shape[0]
    tm = ROW_TILE
    dq = H_R * DK_R
    dv = H_R * DV_R
    wq = w_in[:, :dq].astype(BF16)
    wk = w_in[:, dq:2 * dq].astype(BF16)
    wv = w_in[:, 2 * dq:2 * dq + dv].astype(BF16)
    wg = w_in[:, 2 * dq + dv:].astype(BF16)
    half = DK_R // 2
    inv = (RET_THETA ** (-jnp.arange(half, dtype=F32) / half))[None, :]
    row = lambda width: pl.BlockSpec((tm, width), lambda i: (i, 0))
    outs = [(n, dq), (n, dq), (n, dv), (n, dv)]
    return pl.pallas_call(
        _ret_in_kernel,
        grid=(n // tm,),
        in_specs=[row(D_MODEL), row(1), _const_spec((1, D_MODEL)),
                  _const_spec(wq.shape), _const_spec(wk.shape), _const_spec(wv.shape),
                  _const_spec(wg.shape), _const_spec((1, half))],
        out_specs=[row(s[1]) for s in outs],
        out_shape=[jax.ShapeDtypeStruct(s, BF16) for s in outs],
        compiler_params=pltpu.CompilerParams(dimension_semantics=("parallel",),
                                             vmem_limit_bytes=VMEM_LIMIT_BYTES),
        name="ret_in_proj",
    )(x2, pos2, g[None, :], wq, wk, wv, wg, inv)


def _retention_kernel(lg_ref, q_ref, k_ref, v_ref, gate_ref, gn_ref, y_ref, state_ref):
    c = q_ref.shape[0]

    @pl.when(pl.program_id(2) == 0)
    def _():
        state_ref[...] = jnp.zeros(state_ref.shape, F32)

    lg = lg_ref[0:1, 0:1]
    i_col = lax.broadcasted_iota(jnp.int32, (c, 1), 0).astype(F32)
    j_row = lax.broadcasted_iota(jnp.int32, (1, c), 1).astype(F32)
    diff = i_col - j_row
    decay = jnp.where(diff >= 0.0, jnp.exp(jnp.maximum(diff, 0.0) * lg), 0.0)
    xi = jnp.exp((i_col + 1.0) * lg)
    zeta = jnp.exp((float(c) - 1.0 - i_col) * lg)
    gamma_c = jnp.exp(float(c) * lg)

    q = q_ref[...]
    k = k_ref[...]
    v = v_ref[...]
    state = state_ref[...]
    inner = _dot_nt(q, k) * decay
    out = _dot(inner.astype(BF16), v) + _dot((q.astype(F32) * xi).astype(BF16), state.astype(BF16))
    state_ref[...] = gamma_c * state + _dot_tn((k.astype(F32) * zeta).astype(BF16), v)

    mu = jnp.mean(out, axis=-1, keepdims=True)
    d = out - mu
    var = jnp.mean(d * d, axis=-1, keepdims=True)
    yn = d * lax.rsqrt(var + RMS_EPS) * gn_ref[...]
    gate = gate_ref[...].astype(F32)
    y_ref[...] = (gate * jax.nn.sigmoid(gate) * yn).astype(BF16)


def _retention(q, k, v, gate, gn_g):
    b, s, _ = q.shape
    c = RET_CHUNK
    log_gamma = jnp.log1p(-jnp.exp2(-5.0 - jnp.arange(H_R, dtype=F32)))
    lg_tab = jnp.broadcast_to(log_gamma[:, None, None], (H_R, 8, LANES))
    qk_spec = pl.BlockSpec((None, c, DK_R), lambda bi, h, ci: (bi, ci, h))
    v_spec = pl.BlockSpec((None, c, DV_R), lambda bi, h, ci: (bi, ci, h))
    return pl.pallas_call(
        _retention_kernel,
        grid=(b, H_R, s // c),
        in_specs=[pl.BlockSpec((None, 8, LANES), lambda bi, h, ci: (h, 0, 0)),
                  qk_spec, qk_spec, v_spec, v_spec,
                  pl.BlockSpec((1, DV_R), lambda bi, h, ci: (0, h))],
        out_specs=v_spec,
        out_shape=jax.ShapeDtypeStruct((b, s, H_R * DV_R), BF16),
        scratch_shapes=[pltpu.VMEM((DK_R, DV_R), F32)],
        compiler_params=pltpu.CompilerParams(
            dimension_semantics=("parallel", "parallel", "arbitrary"),
            vmem_limit_bytes=VMEM_LIMIT_BYTES),
        name="retention",
    )(lg_tab, q, k, v, gate, gn_g[None, :])


def kernel(x, positions, norm_mix_g, norm_mlp_g, w_in_a, w_out_a, w_in_b, ret_norm_g, w_out_b,
           w_mlp_up, w_mlp_down, final_norm_g):
    b, s, d = x.shape
    n = b * s
    x2 = x.reshape(n, d)
    pos2 = positions.reshape(n, 1)

    q, k, vt, qi, kilo, kihi, w = _dsa_in_proj(x2, pos2, norm_mix_g[0], w_in_a[0], b, s)
    r3 = lambda a: a.reshape(b, s, a.shape[-1])
    o = _dsa_attention(q, r3(qi), r3(w), r3(k), vt, r3(kilo), r3(kihi))
    h = _out_mlp(x2, o.reshape(n, -1), w_out_a[0], norm_mlp_g[0], w_mlp_up[0], w_mlp_down[0],
                 final_norm_g, final_norm=False)

    rq, rk, rv, rg = _ret_in_proj(h, pos2, norm_mix_g[1], w_in_b[0])
    y = _retention(r3(rq), r3(rk), r3(rv), r3(rg), ret_norm_g[0])
    out = _out_mlp(h, y.reshape(n, -1), w_out_b[0], norm_mlp_g[1], w_mlp_up[1], w_mlp_down[1],
                   final_norm_g, final_norm=True)
    return out.reshape(b, s, d)
```

```python
import functools
import math

import jax
import jax.numpy as jnp
import numpy as np
from jax import lax
from jax.experimental import pallas as pl
from jax.experimental.pallas import tpu as pltpu

D_MODEL = 1024
RMS_EPS = 1e-6
ROPE_THETA = 500000.0
H_A = 8
DH_A = D_MODEL // H_A
ROT_A = DH_A // 4
H_IDX = 8
D_IDX = 64
ROT_IDX = D_IDX // 4
TOPK_MAX = 256
H_R = 4
DK_R = D_MODEL // H_R
DV_R = 2 * DK_R
RET_THETA = 10000.0
D_FF = 4 * D_MODEL

LANES = 128
VMEM_LIMIT_BYTES = 56 * 1024 * 1024

ROW_TILE = 512
FF_TILE = 1024
Q_TILE = 256
K_TILE = 512
COUNT_ROWS = 64
RET_CHUNK = 256

NEG_BIG = -1e30
Q_SCALE = math.log2(math.e) / math.sqrt(DH_A)
INT_MIN = np.int32(-2 ** 31)
KEY_NEG_INF = np.int32(-2 ** 31 + 0x7FFFFF)
MANT_MASK = np.int32(0x7FFFFFFF)

BF16 = jnp.bfloat16
F32 = jnp.float32


def _dot(a, b):
    return jnp.dot(a, b, preferred_element_type=F32)


def _dot_nt(a, b):
    return lax.dot_general(a, b, (((1,), (1,)), ((), ())), preferred_element_type=F32)


def _dot_tn(a, b):
    return lax.dot_general(a, b, (((0,), (0,)), ((), ())), preferred_element_type=F32)


def _rms(x, g):
    ms = jnp.mean(x * x, axis=-1, keepdims=True)
    return x * lax.rsqrt(ms + RMS_EPS) * g


def _const_spec(shape):
    nd = len(shape)
    return pl.BlockSpec(shape, lambda *_: (0,) * nd, pipeline_mode=pl.Buffered(1))


def _dsa_in_kernel(x_ref, pos_ref, g_ref, wq_ref, wkv_ref, wqi_ref, wki_ref, ww_ref,
                   inv_a_ref, sgn_a_ref, inv_i_ref, sgn_i_ref,
                   q_ref, k_ref, vt_ref, qi_ref, kilo_ref, kihi_ref, w_ref):
    hn = _rms(x_ref[...], g_ref[...]).astype(BF16)
    pos = pos_ref[...].astype(F32)
    lane = lax.broadcasted_iota(jnp.int32, (1, LANES), 1)

    ang_a = pos * inv_a_ref[...]
    cos_a = jnp.cos(ang_a)
    sin_a = jnp.sin(ang_a) * sgn_a_ref[...]
    first_a = lane < (ROT_A // 2)

    def rot_a(y):
        sw = jnp.where(first_a, pltpu.roll(y, LANES - ROT_A // 2, 1), pltpu.roll(y, ROT_A // 2, 1))
        return y * cos_a + sw * sin_a

    ang_i = pos * inv_i_ref[...]
    cos_i = jnp.cos(ang_i)
    sin_i = jnp.sin(ang_i) * sgn_i_ref[...]
    first_i = (lane & (D_IDX - 1)) < (ROT_IDX // 2)

    def rot_i(y):
        sw = jnp.where(first_i, pltpu.roll(y, LANES - ROT_IDX // 2, 1), pltpu.roll(y, ROT_IDX // 2, 1))
        return y * cos_i + sw * sin_i

    for c in range(H_A // 2):
        y = _dot(hn, wq_ref[:, c * 2 * DH_A:(c + 1) * 2 * DH_A])
        for t in range(2):
            h = 2 * c + t
            q_ref[2 * c + t] = (rot_a(y[:, t * DH_A:(t + 1) * DH_A]) * Q_SCALE).astype(BF16)

    kv = _dot(hn, wkv_ref[...])
    k_ref[...] = rot_a(kv[:, :DH_A]).astype(BF16)
    vt_ref[...] = kv[:, DH_A:].T.astype(BF16)

    for c in range(H_IDX * D_IDX // (2 * LANES)):
        y = _dot(hn, wqi_ref[:, c * 2 * LANES:(c + 1) * 2 * LANES])
        for t in range(2):
            j = 2 * c + t
            qi_ref[:, j * LANES:(j + 1) * LANES] = rot_i(y[:, t * LANES:(t + 1) * LANES]).astype(BF16)

    ki = _dot(hn, wki_ref[...])
    kilo_ref[...] = rot_i(ki[:, :LANES]).astype(BF16)
    kihi_ref[...] = rot_i(ki[:, LANES:]).astype(BF16)

    w_ref[...] = _dot(hn, ww_ref[...]) * (H_IDX ** -0.5 * D_IDX ** -0.5)


def _dsa_in_proj(x2, pos2, g, w_in, b, s):
    n = x2.shape[0]
    tm = ROW_TILE
    tiles_per_seq = s // tm
    o_q = H_A * DH_A
    o_k = o_q + DH_A
    o_v = o_k + DH_A
    o_qi = o_v + H_IDX * D_IDX
    o_ki = o_qi + D_IDX
    wq = w_in[:, :o_q].astype(BF16)
    wkv = w_in[:, o_q:o_v].astype(BF16)
    wqi = w_in[:, o_v:o_qi].astype(BF16)
    wki_cols = w_in[:, o_qi:o_ki]
    zeros = jnp.zeros_like(wki_cols)
    wki = jnp.concatenate([wki_cols, zeros, zeros, wki_cols], axis=1).astype(BF16)
    ww = jnp.pad(w_in[:, o_ki:], ((0, 0), (0, LANES - H_IDX))).astype(BF16)

    def pattern(rot, period):
        half = rot // 2
        inv = ROPE_THETA ** (-jnp.arange(half, dtype=F32) / half)
        inv_p = jnp.concatenate([inv, inv, jnp.zeros((period - rot,), F32)])
        sgn_p = jnp.concatenate([-jnp.ones((half,), F32), jnp.ones((half,), F32),
                                 jnp.zeros((period - rot,), F32)])
        reps = LANES // period
        return jnp.tile(inv_p, reps)[None, :], jnp.tile(sgn_p, reps)[None, :]

    inv_a, sgn_a = pattern(ROT_A, DH_A)
    inv_i, sgn_i = pattern(ROT_IDX, D_IDX)

    row = lambda width: pl.BlockSpec((tm, width), lambda i: (i, 0))
    rows_out = lambda width, dt: (row(width), jax.ShapeDtypeStruct((n, width), dt))
    vt_out = (pl.BlockSpec((None, DH_A, tm), lambda i: (i // tiles_per_seq, 0, i % tiles_per_seq)),
              jax.ShapeDtypeStruct((b, DH_A, s), BF16))
    q_out = (pl.BlockSpec((None, H_A, tm, DH_A),
                          lambda i: (i // tiles_per_seq, 0, i % tiles_per_seq, 0)),
             jax.ShapeDtypeStruct((b, H_A, s, DH_A), BF16))
    outs = [q_out, rows_out(DH_A, BF16), vt_out, rows_out(H_IDX * D_IDX, BF16),
            rows_out(LANES, BF16), rows_out(LANES, BF16), rows_out(LANES, F32)]
    return pl.pallas_call(
        _dsa_in_kernel,
        grid=(n // tm,),
        in_specs=[row(D_MODEL), row(1), _const_spec((1, D_MODEL)),
                  _const_spec(wq.shape), _const_spec(wkv.shape), _const_spec(wqi.shape),
                  _const_spec(wki.shape), _const_spec(ww.shape),
                  _const_spec((1, LANES)), _const_spec((1, LANES)),
                  _const_spec((1, LANES)), _const_spec((1, LANES))],
        out_specs=[o[0] for o in outs],
        out_shape=[o[1] for o in outs],
        compiler_params=pltpu.CompilerParams(dimension_semantics=("parallel",),
                                             vmem_limit_bytes=VMEM_LIMIT_BYTES),
        name="dsa_in_proj",
    )(x2, pos2, g[None, :], wq, wkv, wqi, wki, ww, inv_a, sgn_a, inv_i, sgn_i)


def _key_to_float(key):
    key = jnp.maximum(key, KEY_NEG_INF)
    bits = jnp.where(key >= 0, key, key ^ MANT_MASK)
    return lax.bitcast_convert_type(bits, F32)


def _sum_row_groups(x, n_acc=4):
    groups = [x[g * 8:(g + 1) * 8, :] for g in range(x.shape[0] // 8)]
    accs = groups[:n_acc]
    for i, g in enumerate(groups[n_acc:]):
        accs[i % n_acc] = accs[i % n_acc] + g
    while len(accs) > 1:
        accs = [a + b for a, b in zip(accs[::2], accs[1::2])]
    return accs[0]


def _dsa_attn_kernel(q_ref, qi_ref, w_ref, k_ref, vt_ref, kilo_ref, kihi_ref, o_ref,
                     sc_ref, m_ref, l_ref, acc_ref, cut_ref, *, topk):
    tq = qi_ref.shape[0]
    s_len = k_ref.shape[0]
    q0 = pl.program_id(1) * tq
    n_ck = lax.div(q0 + (tq + K_TILE - 1), jnp.int32(K_TILE))
    query_t = q0 + lax.broadcasted_iota(jnp.int32, (1, tq), 1)
    key_i = lax.broadcasted_iota(jnp.int32, (K_TILE, 1), 0)
    w_t = w_ref[...].T

    def score_body(j, carry):
        off = pl.multiple_of(j * K_TILE, K_TILE)
        klo = kilo_ref[pl.ds(off, K_TILE), :]
        khi = kihi_ref[pl.ds(off, K_TILE), :]
        acc = jnp.zeros((K_TILE, tq), F32)
        for p in range(H_IDX // 2):
            a = qi_ref[:, p * LANES:(p + 1) * LANES]
            acc += jnp.maximum(_dot_nt(klo, a), 0.0) * w_t[2 * p:2 * p + 1, :]
            acc += jnp.maximum(_dot_nt(khi, a), 0.0) * w_t[2 * p + 1:2 * p + 2, :]
        sc_ref[pl.ds(off, K_TILE), :] = jnp.where(off + key_i <= query_t, acc, -jnp.inf)
        return carry

    lax.fori_loop(0, n_ck, score_body, 0)

    def count_ge(thr):
        def body(j, part):
            off = pl.multiple_of(j * K_TILE, K_TILE)
            for r in range(K_TILE // COUNT_ROWS):
                sc = sc_ref[pl.ds(off + r * COUNT_ROWS, COUNT_ROWS), :]
                part = part + _sum_row_groups(jnp.where(sc >= thr, 1.0, 0.0))
            return part

        part = lax.fori_loop(0, n_ck, body, jnp.zeros((8, tq), F32))
        return jnp.sum(part, axis=0, keepdims=True)

    def search_body(it, carry):
        ukey, cnt_in, cnt_out = carry
        cand = ukey | lax.shift_left(jnp.int32(1), 31 - it)
        cnt = count_ge(_key_to_float(cand ^ INT_MIN))
        ok = cnt >= float(topk)
        return (jnp.where(ok, cand, ukey), jnp.where(ok, cnt, cnt_in), jnp.where(ok, cnt_out, cnt))

    zeros = jnp.zeros((1, tq), F32)
    ukey, cnt_ge, cnt_gt = lax.fori_loop(
        0, 32, search_body, (jnp.zeros((1, tq), jnp.int32), zeros + float(s_len), zeros))
    key = ukey ^ INT_MIN
    thr = _key_to_float(key)
    thr_up = _key_to_float(key + 1)
    need = float(topk) - cnt_gt

    cut_ref[...] = jnp.full(cut_ref.shape, s_len, jnp.int32)
    excess = jnp.where((cnt_ge - cnt_gt > need) & (key > KEY_NEG_INF), 1, 0)

    @pl.when(jnp.max(excess) > 0)
    def _():
        def count_ties_before(c):
            def body(j, part):
                off = pl.multiple_of(j * K_TILE, K_TILE)
                sc = sc_ref[pl.ds(off, K_TILE), :]
                tie = (sc >= thr) & jnp.logical_not(sc >= thr_up) & (off + key_i < c)
                return part + _sum_row_groups(jnp.where(tie, 1.0, 0.0))

            part = lax.fori_loop(0, n_ck, body, jnp.zeros((8, tq), F32))
            return jnp.sum(part, axis=0, keepdims=True)

        def cut_body(it, c):
            cand = c | lax.shift_left(jnp.int32(1), (s_len.bit_length() - 1) - it)
            return jnp.where(count_ties_before(cand) <= need - 1.0, cand, c)

        c = lax.fori_loop(0, s_len.bit_length(), cut_body, jnp.zeros((1, tq), jnp.int32))
        cut_ref[...] = jnp.broadcast_to(c, cut_ref.shape)

    cut = cut_ref[0:1, :]

    m_ref[...] = jnp.full(m_ref.shape, NEG_BIG, F32)
    l_ref[...] = jnp.zeros(l_ref.shape, F32)
    acc_ref[...] = jnp.zeros(acc_ref.shape, F32)

    def attn_body(j, carry):
        off = pl.multiple_of(j * K_TILE, K_TILE)
        sc = sc_ref[pl.ds(off, K_TILE), :]
        idx = off + key_i
        sel = (sc >= thr_up) | ((sc >= thr) & (idx <= cut))
        bias = jnp.where(sel & (idx <= query_t), 0.0, NEG_BIG)
        kc = k_ref[pl.ds(off, K_TILE), :]
        vtc = vt_ref[:, pl.ds(off, K_TILE)]
        q_all = q_ref[...].reshape(H_A * tq, DH_A)
        lg_all = _dot_nt(kc, q_all)
        m_prev = m_ref[...]
        l_prev = l_ref[...]
        ps, m_news, l_news = [], [], []
        for h in range(H_A):
            cols = slice(h * tq, (h + 1) * tq)
            lg = lg_all[:, cols] + bias
            m_new = jnp.maximum(m_prev[:, cols], jnp.max(lg, axis=0, keepdims=True))
            p = jnp.exp2(lg - m_new)
            l_news.append(jnp.exp2(m_prev[:, cols] - m_new) * l_prev[:, cols]
                          + jnp.sum(p, axis=0, keepdims=True))
            m_news.append(m_new)
            ps.append(p.astype(BF16))
        m_new = jnp.concatenate(m_news, axis=1)
        alpha = jnp.exp2(m_prev - m_new)
        l_ref[...] = jnp.concatenate(l_news, axis=1)
        acc_ref[...] = alpha * acc_ref[...] + _dot(vtc, jnp.concatenate(ps, axis=1))
        m_ref[...] = m_new
        return carry

    lax.fori_loop(0, n_ck, attn_body, 0)

    out_t = acc_ref[...] / l_ref[...]
    for h in range(H_A):
        o_ref[:, h * DH_A:(h + 1) * DH_A] = out_t[:, h * tq:(h + 1) * tq].T.astype(BF16)


def _dsa_attention(q, qi, w, k, vt, kilo, kihi):
    b, s, _ = k.shape
    tq = Q_TILE
    topk = min(TOPK_MAX, s // 4)
    qt = lambda width: pl.BlockSpec((None, tq, width), lambda bi, qi_: (bi, qi_, 0))
    full = lambda width: pl.BlockSpec((None, s, width), lambda bi, qi_: (bi, 0, 0))
    return pl.pallas_call(
        functools.partial(_dsa_attn_kernel, topk=topk),
        grid=(b, s // tq),
        in_specs=[pl.BlockSpec((None, H_A, tq, DH_A), lambda bi, qi_: (bi, 0, qi_, 0)),
                  qt(H_IDX * D_IDX), qt(LANES), full(DH_A),
                  pl.BlockSpec((None, DH_A, s), lambda bi, qi_: (bi, 0, 0)),
                  full(LANES), full(LANES)],
        out_specs=qt(H_A * DH_A),
        out_shape=jax.ShapeDtypeStruct((b, s, H_A * DH_A), BF16),
        scratch_shapes=[pltpu.VMEM((s, tq), F32),
                        pltpu.VMEM((1, H_A * tq), F32),
                        pltpu.VMEM((1, H_A * tq), F32),
                        pltpu.VMEM((DH_A, H_A * tq), F32),
                        pltpu.VMEM((8, tq), jnp.int32)],
        compiler_params=pltpu.CompilerParams(dimension_semantics=("parallel", "parallel"),
                                             vmem_limit_bytes=VMEM_LIMIT_BYTES),
        name="dsa_attention",
    )(q, qi, w, k, vt, kilo, kihi)


def _out_mlp_kernel(x_ref, y_ref, wo_ref, g_ref, wup_ref, wdn_ref, gf_ref, o_ref, *, final_norm):
    h = x_ref[...] + _dot(y_ref[...], wo_ref[...])
    hn = _rms(h, g_ref[...]).astype(BF16)
    acc = jnp.zeros(h.shape, F32)
    for c in range(D_FF // FF_TILE):
        u = jnp.maximum(_dot(hn, wup_ref[:, c * FF_TILE:(c + 1) * FF_TILE]), 0.0)
        acc += _dot((u * u).astype(BF16), wdn_ref[c * FF_TILE:(c + 1) * FF_TILE, :])
    h = h + acc
    if final_norm:
        h = _rms(h, gf_ref[...])
    o_ref[...] = h


def _out_mlp(x2, y2, w_out, g_mlp, w_up, w_down, g_final, final_norm):
    n = x2.shape[0]
    tm = ROW_TILE
    kin = y2.shape[1]
    row = lambda width: pl.BlockSpec((tm, width), lambda i: (i, 0))
    return pl.pallas_call(
        functools.partial(_out_mlp_kernel, final_norm=final_norm),
        grid=(n // tm,),
        in_specs=[row(D_MODEL), row(kin), _const_spec((kin, D_MODEL)), _const_spec((1, D_MODEL)),
                  _const_spec((D_MODEL, D_FF)), _const_spec((D_FF, D_MODEL)),
                  _const_spec((1, D_MODEL))],
        out_specs=row(D_MODEL),
        out_shape=jax.ShapeDtypeStruct((n, D_MODEL), F32),
        compiler_params=pltpu.CompilerParams(dimension_semantics=("parallel",),
                                             vmem_limit_bytes=VMEM_LIMIT_BYTES),
        name="out_proj_mlp_final" if final_norm else "out_proj_mlp",
    )(x2, y2, w_out.astype(BF16), g_mlp[None, :], w_up.astype(BF16), w_down.astype(BF16),
      g_final[None, :])


def _ret_in_kernel(x_ref, pos_ref, g_ref, wq_ref, wk_ref, wv_ref, wg_ref, inv_ref,
                   q_ref, k_ref, v_ref, gate_ref):
    hn = _rms(x_ref[...], g_ref[...]).astype(BF16)
    ang = pos_ref[...].astype(F32) * inv_ref[...]
    cos = jnp.cos(ang)
    sin = jnp.sin(ang)
    half = DK_R // 2

    def rot_store(w_ref, out_ref, mul):
        for h in range(H_R):
            y = _dot(hn, w_ref[:, h * DK_R:(h + 1) * DK_R])
            x1 = y[:, :half]
            x2 = y[:, half:]
            out_ref[:, h * DK_R:h * DK_R + half] = ((x1 * cos - x2 * sin) * mul).astype(BF16)
            out_ref[:, h * DK_R + half:(h + 1) * DK_R] = ((x2 * cos + x1 * sin) * mul).astype(BF16)

    rot_store(wq_ref, q_ref, 1.0)
    rot_store(wk_ref, k_ref, DK_R ** -0.5)
    for c in range(H_R):
        v_ref[:, c * DV_R:(c + 1) * DV_R] = _dot(hn, wv_ref[:, c * DV_R:(c + 1) * DV_R]).astype(BF16)
        gate_ref[:, c * DV_R:(c + 1) * DV_R] = _dot(hn, wg_ref[:, c * DV_R:(c + 1) * DV_R]).astype(BF16)


def _ret_in_proj(x2, pos2, g, w_in):
    n = x2.shape[0]
    tm = ROW_TILE
    dq = H_R * DK_R
    dv = H_R * DV_R
    wq = w_in[:, :dq].astype(BF16)
    wk = w_in[:, dq:2 * dq].astype(BF16)
    wv = w_in[:, 2 * dq:2 * dq + dv].astype(BF16)
    wg = w_in[:, 2 * dq + dv:].astype(BF16)
    half = DK_R // 2
    inv = (RET_THETA ** (-jnp.arange(half, dtype=F32) / half))[None, :]
    row = lambda width: pl.BlockSpec((tm, width), lambda i: (i, 0))
    outs = [(n, dq), (n, dq), (n, dv), (n, dv)]
    return pl.pallas_call(
        _ret_in_kernel,
        grid=(n // tm,),
        in_specs=[row(D_MODEL), row(1), _const_spec((1, D_MODEL)),
                  _const_spec(wq.shape), _const_spec(wk.shape), _const_spec(wv.shape),
                  _const_spec(wg.shape), _const_spec((1, half))],
        out_specs=[row(s[1]) for s in outs],
        out_shape=[jax.ShapeDtypeStruct(s, BF16) for s in outs],
        compiler_params=pltpu.CompilerParams(dimension_semantics=("parallel",),
                                             vmem_limit_bytes=VMEM_LIMIT_BYTES),
        name="ret_in_proj",
    )(x2, pos2, g[None, :], wq, wk, wv, wg, inv)


def _retention_kernel(lg_ref, q_ref, k_ref, v_ref, gate_ref, gn_ref, y_ref, state_ref):
    c = q_ref.shape[0]

    @pl.when(pl.program_id(2) == 0)
    def _():
        state_ref[...] = jnp.zeros(state_ref.shape, F32)

    lg = lg_ref[0:1, 0:1]
    i_col = lax.broadcasted_iota(jnp.int32, (c, 1), 0).astype(F32)
    j_row = lax.broadcasted_iota(jnp.int32, (1, c), 1).astype(F32)
    diff = i_col - j_row
    decay = jnp.where(diff >= 0.0, jnp.exp(jnp.maximum(diff, 0.0) * lg), 0.0)
    xi = jnp.exp((i_col + 1.0) * lg)
    zeta = jnp.exp((float(c) - 1.0 - i_col) * lg)
    gamma_c = jnp.exp(float(c) * lg)

    q = q_ref[...]
    k = k_ref[...]
    v = v_ref[...]
    state = state_ref[...]
    inner = _dot_nt(q, k) * decay
    out = _dot(inner.astype(BF16), v) + _dot((q.astype(F32) * xi).astype(BF16), state.astype(BF16))
    state_ref[...] = gamma_c * state + _dot_tn((k.astype(F32) * zeta).astype(BF16), v)

    mu = jnp.mean(out, axis=-1, keepdims=True)
    d = out - mu
    var = jnp.mean(d * d, axis=-1, keepdims=True)
    yn = d * lax.rsqrt(var + RMS_EPS) * gn_ref[...]
    gate = gate_ref[...].astype(F32)
    y_ref[...] = (gate * jax.nn.sigmoid(gate) * yn).astype(BF16)


def _retention(q, k, v, gate, gn_g):
    b, s, _ = q.shape
    c = RET_CHUNK
    log_gamma = jnp.log1p(-jnp.exp2(-5.0 - jnp.arange(H_R, dtype=F32)))
    lg_tab = jnp.broadcast_to(log_gamma[:, None, None], (H_R, 8, LANES))
    qk_spec = pl.BlockSpec((None, c, DK_R), lambda bi, h, ci: (bi, ci, h))
    v_spec = pl.BlockSpec((None, c, DV_R), lambda bi, h, ci: (bi, ci, h))
    return pl.pallas_call(
        _retention_kernel,
        grid=(b, H_R, s // c),
        in_specs=[pl.BlockSpec((None, 8, LANES), lambda bi, h, ci: (h, 0, 0)),
                  qk_spec, qk_spec, v_spec, v_spec,
                  pl.BlockSpec((1, DV_R), lambda bi, h, ci: (0, h))],
        out_specs=v_spec,
        out_shape=jax.ShapeDtypeStruct((b, s, H_R * DV_R), BF16),
        scratch_shapes=[pltpu.VMEM((DK_R, DV_R), F32)],
        compiler_params=pltpu.CompilerParams(
            dimension_semantics=("parallel", "parallel", "arbitrary"),
            vmem_limit_bytes=VMEM_LIMIT_BYTES),
        name="retention",
    )(lg_tab, q, k, v, gate, gn_g[None, :])


def kernel(x, positions, norm_mix_g, norm_mlp_g, w_in_a, w_out_a, w_in_b, ret_norm_g, w_out_b,
           w_mlp_up, w_mlp_down, final_norm_g):
    b, s, d = x.shape
    n = b * s
    x2 = x.reshape(n, d)
    pos2 = positions.reshape(n, 1)

    q, k, vt, qi, kilo, kihi, w = _dsa_in_proj(x2, pos2, norm_mix_g[0], w_in_a[0], b, s)
    r3 = lambda a: a.reshape(b, s, a.shape[-1])
    o = _dsa_attention(q, r3(qi), r3(w), r3(k), vt, r3(kilo), r3(kihi))
    h = _out_mlp(x2, o.reshape(n, -1), w_out_a[0], norm_mlp_g[0], w_mlp_up[0], w_mlp_down[0],
                 final_norm_g, final_norm=False)

    rq, rk, rv, rg = _ret_in_proj(h, pos2, norm_mix_g[1], w_in_b[0])
    y = _retention(r3(rq), r3(rk), r3(rv), r3(rg), ret_norm_g[0])
    out = _out_mlp(h, y.reshape(n, -1), w_out_b[0], norm_mlp_g[1], w_mlp_up[1], w_mlp_down[1],
                   final_norm_g, final_norm=True)
    return out.reshape(b, s, d)
```

```python
import functools
import math

import jax
import jax.numpy as jnp
import numpy as np
from jax import lax
from jax.experimental import pallas as pl
from jax.experimental.pallas import tpu as pltpu

D_MODEL = 1024
RMS_EPS = 1e-6
ROPE_THETA = 500000.0
H_A = 8
DH_A = D_MODEL // H_A
ROT_A = DH_A // 4
H_IDX = 8
D_IDX = 64
ROT_IDX = D_IDX // 4
TOPK_MAX = 256
H_R = 4
DK_R = D_MODEL // H_R
DV_R = 2 * DK_R
RET_THETA = 10000.0
D_FF = 4 * D_MODEL

LANES = 128
VMEM_LIMIT_BYTES = 56 * 1024 * 1024

ROW_TILE = 512
FF_TILE = 1024
Q_TILE = 256
K_TILE = 512
COUNT_ROWS = 64
COUNT16_ROWS = 128
RET_CHUNK = 256

NEG_BIG = -1e30
NORM_SLACK = 1.02
UNDERFLOW_GUARD = 2.0 ** -80
Q_SCALE = math.log2(math.e) / math.sqrt(DH_A)
INT_MIN = np.int32(-2 ** 31)
KEY_NEG_INF = np.int32(-2 ** 31 + 0x7FFFFF)
MANT_MASK = np.int32(0x7FFFFFFF)

BF16 = jnp.bfloat16
F32 = jnp.float32


def _dot(a, b):
    return jnp.dot(a, b, preferred_element_type=F32)


def _dot_nt(a, b):
    return lax.dot_general(a, b, (((1,), (1,)), ((), ())), preferred_element_type=F32)


def _dot_tn(a, b):
    return lax.dot_general(a, b, (((0,), (0,)), ((), ())), preferred_element_type=F32)


def _rms(x, g):
    ms = jnp.mean(x * x, axis=-1, keepdims=True)
    return x * lax.rsqrt(ms + RMS_EPS) * g


def _const_spec(shape):
    nd = len(shape)
    return pl.BlockSpec(shape, lambda *_: (0,) * nd, pipeline_mode=pl.Buffered(1))


def _dsa_in_kernel(x_ref, pos_ref, g_ref, wq_ref, wkv_ref, wqi_ref, wki_ref, ww_ref,
                   inv_a_ref, sgn_a_ref, inv_i_ref, sgn_i_ref,
                   q_ref, k_ref, ksq_ref, vt_ref, qi_ref, kilo_ref, kihi_ref, w_ref):
    hn = _rms(x_ref[...], g_ref[...]).astype(BF16)
    pos = pos_ref[...].astype(F32)
    lane = lax.broadcasted_iota(jnp.int32, (1, LANES), 1)

    ang_a = pos * inv_a_ref[...]
    cos_a = jnp.cos(ang_a)
    sin_a = jnp.sin(ang_a) * sgn_a_ref[...]
    first_a = lane < (ROT_A // 2)

    def rot_a(y):
        sw = jnp.where(first_a, pltpu.roll(y, LANES - ROT_A // 2, 1), pltpu.roll(y, ROT_A // 2, 1))
        return y * cos_a + sw * sin_a

    ang_i = pos * inv_i_ref[...]
    cos_i = jnp.cos(ang_i)
    sin_i = jnp.sin(ang_i) * sgn_i_ref[...]
    first_i = (lane & (D_IDX - 1)) < (ROT_IDX // 2)

    def rot_i(y):
        sw = jnp.where(first_i, pltpu.roll(y, LANES - ROT_IDX // 2, 1), pltpu.roll(y, ROT_IDX // 2, 1))
        return y * cos_i + sw * sin_i

    for c in range(H_A // 2):
        y = _dot(hn, wq_ref[:, c * 2 * DH_A:(c + 1) * 2 * DH_A])
        for t in range(2):
            h = 2 * c + t
            q_ref[2 * c + t] = (rot_a(y[:, t * DH_A:(t + 1) * DH_A]) * Q_SCALE).astype(BF16)

    kv = _dot(hn, wkv_ref[...])
    k_rot = rot_a(kv[:, :DH_A]).astype(BF16)
    k_ref[...] = k_rot
    k_f32 = k_rot.astype(F32)
    ksq_ref[...] = _dot_nt(jnp.ones((8, DH_A), BF16), (k_f32 * k_f32).astype(BF16))
    vt_ref[...] = kv[:, DH_A:].T.astype(BF16)

    for c in range(H_IDX * D_IDX // (2 * LANES)):
        y = _dot(hn, wqi_ref[:, c * 2 * LANES:(c + 1) * 2 * LANES])
        for t in range(2):
            j = 2 * c + t
            qi_ref[:, j * LANES:(j + 1) * LANES] = rot_i(y[:, t * LANES:(t + 1) * LANES]).astype(BF16)

    ki = _dot(hn, wki_ref[...])
    kilo_ref[...] = rot_i(ki[:, :LANES]).astype(BF16)
    kihi_ref[...] = rot_i(ki[:, LANES:]).astype(BF16)

    w_ref[...] = _dot(hn, ww_ref[...]) * (H_IDX ** -0.5 * D_IDX ** -0.5)


def _dsa_in_proj(x2, pos2, g, w_in, b, s):
    n = x2.shape[0]
    tm = ROW_TILE
    tiles_per_seq = s // tm
    o_q = H_A * DH_A
    o_k = o_q + DH_A
    o_v = o_k + DH_A
    o_qi = o_v + H_IDX * D_IDX
    o_ki = o_qi + D_IDX
    wq = w_in[:, :o_q].astype(BF16)
    wkv = w_in[:, o_q:o_v].astype(BF16)
    wqi = w_in[:, o_v:o_qi].astype(BF16)
    wki_cols = w_in[:, o_qi:o_ki]
    zeros = jnp.zeros_like(wki_cols)
    wki = jnp.concatenate([wki_cols, zeros, zeros, wki_cols], axis=1).astype(BF16)
    ww = jnp.pad(w_in[:, o_ki:], ((0, 0), (0, LANES - H_IDX))).astype(BF16)

    def pattern(rot, period):
        half = rot // 2
        inv = ROPE_THETA ** (-jnp.arange(half, dtype=F32) / half)
        inv_p = jnp.concatenate([inv, inv, jnp.zeros((period - rot,), F32)])
        sgn_p = jnp.concatenate([-jnp.ones((half,), F32), jnp.ones((half,), F32),
                                 jnp.zeros((period - rot,), F32)])
        reps = LANES // period
        return jnp.tile(inv_p, reps)[None, :], jnp.tile(sgn_p, reps)[None, :]

    inv_a, sgn_a = pattern(ROT_A, DH_A)
    inv_i, sgn_i = pattern(ROT_IDX, D_IDX)

    row = lambda width: pl.BlockSpec((tm, width), lambda i: (i, 0))
    rows_out = lambda width, dt: (row(width), jax.ShapeDtypeStruct((n, width), dt))
    vt_out = (pl.BlockSpec((None, DH_A, tm), lambda i: (i // tiles_per_seq, 0, i % tiles_per_seq)),
              jax.ShapeDtypeStruct((b, DH_A, s), BF16))
    q_out = (pl.BlockSpec((None, H_A, tm, DH_A),
                          lambda i: (i // tiles_per_seq, 0, i % tiles_per_seq, 0)),
             jax.ShapeDtypeStruct((b, H_A, s, DH_A), BF16))
    ksq_out = (pl.BlockSpec((None, 8, tm), lambda i: (i // tiles_per_seq, 0, i % tiles_per_seq)),
               jax.ShapeDtypeStruct((b, 8, s), F32))
    outs = [q_out, rows_out(DH_A, BF16), ksq_out, vt_out, rows_out(H_IDX * D_IDX, BF16),
            rows_out(LANES, BF16), rows_out(LANES, BF16), rows_out(LANES, F32)]
    return pl.pallas_call(
        _dsa_in_kernel,
        grid=(n // tm,),
        in_specs=[row(D_MODEL), row(1), _const_spec((1, D_MODEL)),
                  _const_spec(wq.shape), _const_spec(wkv.shape), _const_spec(wqi.shape),
                  _const_spec(wki.shape), _const_spec(ww.shape),
                  _const_spec((1, LANES)), _const_spec((1, LANES)),
                  _const_spec((1, LANES)), _const_spec((1, LANES))],
        out_specs=[o[0] for o in outs],
        out_shape=[o[1] for o in outs],
        compiler_params=pltpu.CompilerParams(dimension_semantics=("parallel",),
                                             vmem_limit_bytes=VMEM_LIMIT_BYTES),
        name="dsa_in_proj",
    )(x2, pos2, g[None, :], wq, wkv, wqi, wki, ww, inv_a, sgn_a, inv_i, sgn_i)


def _key_to_float(key):
    key = jnp.maximum(key, KEY_NEG_INF)
    bits = jnp.where(key >= 0, key, key ^ MANT_MASK)
    return lax.bitcast_convert_type(bits, F32)


def _sum_row_groups(x, n_acc=4, rows=8):
    groups = [x[g * rows:(g + 1) * rows, :] for g in range(x.shape[0] // rows)]
    accs = groups[:n_acc]
    for i, g in enumerate(groups[n_acc:]):
        accs[i % n_acc] = accs[i % n_acc] + g
    while len(accs) > 1:
        accs = [a + b for a, b in zip(accs[::2], accs[1::2])]
    return accs[0]


def _dsa_attn_kernel(q_ref, qi_ref, w_ref, k_ref, ksq_ref, vt_ref, kilo_ref, kihi_ref, o_ref,
                     sc_ref, sc16_ref, m_ref, l_ref, acc_ref, cut_ref, *, topk):
    tq = qi_ref.shape[0]
    s_len = k_ref.shape[0]
    q0 = pl.program_id(1) * tq
    n_ck = lax.div(q0 + (tq + K_TILE - 1), jnp.int32(K_TILE))
    query_t = q0 + lax.broadcasted_iota(jnp.int32, (1, tq), 1)
    key_i = lax.broadcasted_iota(jnp.int32, (K_TILE, 1), 0)
    w_t = w_ref[...].T

    def score_body(j, carry):
        off = pl.multiple_of(j * K_TILE, K_TILE)
        klo = kilo_ref[pl.ds(off, K_TILE), :]
        khi = kihi_ref[pl.ds(off, K_TILE), :]
        acc = jnp.zeros((K_TILE, tq), F32)
        for p in range(H_IDX // 2):
            a = qi_ref[:, p * LANES:(p + 1) * LANES]
            acc += jnp.maximum(_dot_nt(klo, a), 0.0) * w_t[2 * p:2 * p + 1, :]
            acc += jnp.maximum(_dot_nt(khi, a), 0.0) * w_t[2 * p + 1:2 * p + 2, :]
        sc = jnp.where(off + key_i <= query_t, acc, -jnp.inf)
        sc_ref[pl.ds(off, K_TILE), :] = sc
        sc16_ref[pl.ds(off, K_TILE), :] = sc.astype(BF16)
        return carry

    lax.fori_loop(0, n_ck, score_body, 0)

    def count_ge(thr):
        def body(j, part):
            off = pl.multiple_of(j * K_TILE, K_TILE)
            for r in range(K_TILE // COUNT_ROWS):
                sc = sc_ref[pl.ds(off + r * COUNT_ROWS, COUNT_ROWS), :]
                part = part + _sum_row_groups(jnp.where(sc >= thr, 1.0, 0.0))
            return part

        part = lax.fori_loop(0, n_ck, body, jnp.zeros((8, tq), F32))
        return jnp.sum(part, axis=0, keepdims=True)

    def count_ge16(thr16):
        one = jnp.ones((), BF16)
        zero = jnp.zeros((), BF16)

        def body(j, part):
            off = pl.multiple_of(j * K_TILE, K_TILE)
            p16 = None
            for r in range(K_TILE // COUNT16_ROWS):
                sc = sc16_ref[pl.ds(off + r * COUNT16_ROWS, COUNT16_ROWS), :]
                g = _sum_row_groups(jnp.where(sc >= thr16, one, zero), n_acc=2, rows=16)
                p16 = g if p16 is None else p16 + g
            return part + p16.astype(F32)

        part = lax.fori_loop(0, n_ck, body, jnp.zeros((16, tq), F32))
        return jnp.sum(part, axis=0, keepdims=True)

    def bf16_key(hi):
        key = lax.shift_left(hi, 16) ^ INT_MIN
        return jnp.where(key < 0, key | 0xFFFF, key)

    def coarse_body(it, hi):
        cand = hi | lax.shift_left(jnp.int32(1), 15 - it)
        cnt = count_ge16(_key_to_float(bf16_key(cand)).astype(BF16))
        return jnp.where(cnt >= float(topk), cand, hi)

    hi = lax.fori_loop(0, 16, coarse_body, jnp.zeros((1, tq), jnp.int32))
    base = bf16_key(hi) - 32768

    def search_body(it, carry):
        d, cnt_out = carry
        cand = d | lax.shift_left(jnp.int32(1), 16 - it)
        cnt = count_ge(_key_to_float(base + cand))
        ok = cnt >= float(topk)
        return (jnp.where(ok, cand, d), jnp.where(ok, cnt_out, cnt))

    d, cnt_gt = lax.fori_loop(0, 17, search_body,
                              (jnp.zeros((1, tq), jnp.int32), jnp.zeros((1, tq), F32)))
    key = base + d
    thr = _key_to_float(key)
    cnt_ge = count_ge(thr)
    thr_up = _key_to_float(key + 1)
    need = float(topk) - cnt_gt

    cut_ref[...] = jnp.full(cut_ref.shape, s_len, jnp.int32)
    excess = jnp.where((cnt_ge - cnt_gt > need) & (key > KEY_NEG_INF), 1, 0)

    @pl.when(jnp.max(excess) > 0)
    def _():
        def count_ties_before(c):
            def body(j, part):
                off = pl.multiple_of(j * K_TILE, K_TILE)
                sc = sc_ref[pl.ds(off, K_TILE), :]
                tie = (sc >= thr) & jnp.logical_not(sc >= thr_up) & (off + key_i < c)
                return part + _sum_row_groups(jnp.where(tie, 1.0, 0.0))

            part = lax.fori_loop(0, n_ck, body, jnp.zeros((8, tq), F32))
            return jnp.sum(part, axis=0, keepdims=True)

        def cut_body(it, c):
            cand = c | lax.shift_left(jnp.int32(1), (s_len.bit_length() - 1) - it)
            return jnp.where(count_ties_before(cand) <= need - 1.0, cand, c)

        c = lax.fori_loop(0, s_len.bit_length(), cut_body, jnp.zeros((1, tq), jnp.int32))
        cut_ref[...] = jnp.broadcast_to(c, cut_ref.shape)

    cut = cut_ref[0:1, :]

    def key_tile(j):
        off = pl.multiple_of(j * K_TILE, K_TILE)
        sc = sc_ref[pl.ds(off, K_TILE), :]
        idx = off + key_i
        sel = (sc >= thr_up) | ((sc >= thr) & (idx <= cut))
        bias = jnp.where(sel & (idx <= query_t), 0.0, NEG_BIG)
        q_all = q_ref[...].reshape(H_A * tq, DH_A)
        lg_all = _dot_nt(k_ref[pl.ds(off, K_TILE), :], q_all)
        return bias, lg_all, vt_ref[:, pl.ds(off, K_TILE)]

    q_f32 = q_ref[...].reshape(H_A * tq, DH_A).astype(F32)
    q_sq = _dot_nt(jnp.ones((8, DH_A), BF16), (q_f32 * q_f32).astype(BF16))[0:1, :]
    k_sq_max = jnp.max(jnp.max(ksq_ref[...], axis=1, keepdims=True), axis=0, keepdims=True)
    shift = jnp.sqrt(q_sq * k_sq_max) * NORM_SLACK
    l_ref[...] = jnp.zeros(l_ref.shape, F32)
    acc_ref[...] = jnp.zeros(acc_ref.shape, F32)

    def bounded_body(j, carry):
        bias, lg_all, vtc = key_tile(j)
        ps, sums = [], []
        for h in range(H_A):
            cols = slice(h * tq, (h + 1) * tq)
            p = jnp.exp2(lg_all[:, cols] - shift[:, cols] + bias)
            sums.append(jnp.sum(p, axis=0, keepdims=True))
            ps.append(p.astype(BF16))
        l_ref[...] += jnp.concatenate(sums, axis=1)
        acc_ref[...] += _dot(vtc, jnp.concatenate(ps, axis=1))
        return carry

    lax.fori_loop(0, n_ck, bounded_body, 0)

    @pl.when(jnp.min(l_ref[...]) < UNDERFLOW_GUARD)
    def _():
        m_ref[...] = jnp.full(m_ref.shape, NEG_BIG, F32)
        l_ref[...] = jnp.zeros(l_ref.shape, F32)
        acc_ref[...] = jnp.zeros(acc_ref.shape, F32)

        def online_body(j, carry):
            bias, lg_all, vtc = key_tile(j)
            m_prev = m_ref[...]
            l_prev = l_ref[...]
            ps, m_news, l_news = [], [], []
            for h in range(H_A):
                cols = slice(h * tq, (h + 1) * tq)
                lg = lg_all[:, cols] + bias
                m_new = jnp.maximum(m_prev[:, cols], jnp.max(lg, axis=0, keepdims=True))
                p = jnp.exp2(lg - m_new)
                l_news.append(jnp.exp2(m_prev[:, cols] - m_new) * l_prev[:, cols]
                              + jnp.sum(p, axis=0, keepdims=True))
                m_news.append(m_new)
                ps.append(p.astype(BF16))
            m_new = jnp.concatenate(m_news, axis=1)
            l_ref[...] = jnp.concatenate(l_news, axis=1)
            acc_ref[...] = (jnp.exp2(m_prev - m_new) * acc_ref[...]
                            + _dot(vtc, jnp.concatenate(ps, axis=1)))
            m_ref[...] = m_new
            return carry

        lax.fori_loop(0, n_ck, online_body, 0)

    out_t = acc_ref[...] / l_ref[...]
    for h in range(H_A):
        o_ref[:, h * DH_A:(h + 1) * DH_A] = out_t[:, h * tq:(h + 1) * tq].T.astype(BF16)


def _dsa_attention(q, qi, w, k, ksq, vt, kilo, kihi):
    b, s, _ = k.shape
    tq = Q_TILE
    topk = min(TOPK_MAX, s // 4)
    qt = lambda width: pl.BlockSpec((None, tq, width), lambda bi, qi_: (bi, qi_, 0))
    full = lambda width: pl.BlockSpec((None, s, width), lambda bi, qi_: (bi, 0, 0))
    return pl.pallas_call(
        functools.partial(_dsa_attn_kernel, topk=topk),
        grid=(b, s // tq),
        in_specs=[pl.BlockSpec((None, H_A, tq, DH_A), lambda bi, qi_: (bi, 0, qi_, 0)),
                  qt(H_IDX * D_IDX), qt(LANES), full(DH_A),
                  pl.BlockSpec((None, 8, s), lambda bi, qi_: (bi, 0, 0)),
                  pl.BlockSpec((None, DH_A, s), lambda bi, qi_: (bi, 0, 0)),
                  full(LANES), full(LANES)],
        out_specs=qt(H_A * DH_A),
        out_shape=jax.ShapeDtypeStruct((b, s, H_A * DH_A), BF16),
        scratch_shapes=[pltpu.VMEM((s, tq), F32),
                        pltpu.VMEM((s, tq), BF16),
                        pltpu.VMEM((1, H_A * tq), F32),
                        pltpu.VMEM((1, H_A * tq), F32),
                        pltpu.VMEM((DH_A, H_A * tq), F32),
                        pltpu.VMEM((8, tq), jnp.int32)],
        compiler_params=pltpu.CompilerParams(dimension_semantics=("parallel", "parallel"),
                                             vmem_limit_bytes=VMEM_LIMIT_BYTES),
        name="dsa_attention",
    )(q, qi, w, k, ksq, vt, kilo, kihi)


def _out_mlp_kernel(x_ref, y_ref, wo_ref, g_ref, wup_ref, wdn_ref, gf_ref, o_ref, *, final_norm):
    h = x_ref[...] + _dot(y_ref[...], wo_ref[...])
    hn = _rms(h, g_ref[...]).astype(BF16)
    acc = jnp.zeros(h.shape, F32)
    for c in range(D_FF // FF_TILE):
        u = jnp.maximum(_dot(hn, wup_ref[:, c * FF_TILE:(c + 1) * FF_TILE]), 0.0)
        acc += _dot((u * u).astype(BF16), wdn_ref[c * FF_TILE:(c + 1) * FF_TILE, :])
    h = h + acc
    if final_norm:
        h = _rms(h, gf_ref[...])
    o_ref[...] = h


def _out_mlp(x2, y2, w_out, g_mlp, w_up, w_down, g_final, final_norm):
    n = x2.shape[0]
    tm = ROW_TILE
    kin = y2.shape[1]
    row = lambda width: pl.BlockSpec((tm, width), lambda i: (i, 0))
    return pl.pallas_call(
        functools.partial(_out_mlp_kernel, final_norm=final_norm),
        grid=(n // tm,),
        in_specs=[row(D_MODEL), row(kin), _const_spec((kin, D_MODEL)), _const_spec((1, D_MODEL)),
                  _const_spec((D_MODEL, D_FF)), _const_spec((D_FF, D_MODEL)),
                  _const_spec((1, D_MODEL))],
        out_specs=row(D_MODEL),
        out_shape=jax.ShapeDtypeStruct((n, D_MODEL), F32),
        compiler_params=pltpu.CompilerParams(dimension_semantics=("parallel",),
                                             vmem_limit_bytes=VMEM_LIMIT_BYTES),
        name="out_proj_mlp_final" if final_norm else "out_proj_mlp",
    )(x2, y2, w_out.astype(BF16), g_mlp[None, :], w_up.astype(BF16), w_down.astype(BF16),
      g_final[None, :])


def _ret_in_kernel(x_ref, pos_ref, g_ref, wq_ref, wk_ref, wv_ref, wg_ref, inv_ref,
                   q_ref, k_ref, v_ref, gate_ref):
    hn = _rms(x_ref[...], g_ref[...]).astype(BF16)
    ang = pos_ref[...].astype(F32) * inv_ref[...]
    cos = jnp.cos(ang)
    sin = jnp.sin(ang)
    half = DK_R // 2

    def rot_store(w_ref, out_ref, mul):
        for h in range(H_R):
            y = _dot(hn, w_ref[:, h * DK_R:(h + 1) * DK_R])
            x1 = y[:, :half]
            x2 = y[:, half:]
            out_ref[:, h * DK_R:h * DK_R + half] = ((x1 * cos - x2 * sin) * mul).astype(BF16)
            out_ref[:, h * DK_R + half:(h + 1) * DK_R] = ((x2 * cos + x1 * sin) * mul).astype(BF16)

    rot_store(wq_ref, q_ref, 1.0)
    rot_store(wk_ref, k_ref, DK_R ** -0.5)
    for c in range(H_R):
        v_ref[:, c * DV_R:(c + 1) * DV_R] = _dot(hn, wv_ref[:, c * DV_R:(c + 1) * DV_R]).astype(BF16)
        gate_ref[:, c * DV_R:(c + 1) * DV_R] = _dot(hn, wg_ref[:, c * DV_R:(c + 1) * DV_R]).astype(BF16)


def _ret_in_proj(x2, pos2, g, w_in):
    n = x2.shape[0]
    tm = ROW_TILE
    dq = H_R * DK_R
    dv = H_R * DV_R
    wq = w_in[:, :dq].astype(BF16)
    wk = w_in[:, dq:2 * dq].astype(BF16)
    wv = w_in[:, 2 * dq:2 * dq + dv].astype(BF16)
    wg = w_in[:, 2 * dq + dv:].astype(BF16)
    half = DK_R // 2
    inv = (RET_THETA ** (-jnp.arange(half, dtype=F32) / half))[None, :]
    row = lambda width: pl.BlockSpec((tm, width), lambda i: (i, 0))
    outs = [(n, dq), (n, dq), (n, dv), (n, dv)]
    return pl.pallas_call(
        _ret_in_kernel,
        grid=(n // tm,),
        in_specs=[row(D_MODEL), row(1), _const_spec((1, D_MODEL)),
                  _const_spec(wq.shape), _const_spec(wk.shape), _const_spec(wv.shape),
                  _const_spec(wg.shape), _const_spec((1, half))],
        out_specs=[row(s[1]) for s in outs],
        out_shape=[jax.ShapeDtypeStruct(s, BF16) for s in outs],
        compiler_params=pltpu.CompilerParams(dimension_semantics=("parallel",),
                                             vmem_limit_bytes=VMEM_LIMIT_BYTES),
        name="ret_in_proj",
    )(x2, pos2, g[None, :], wq, wk, wv, wg, inv)


def _retention_kernel(decay_ref, xi_ref, zeta_ref, gc_ref, q_ref, k_ref, v_ref, gate_ref, gn_ref,
                      y_ref, state_ref):
    @pl.when(pl.program_id(1) == 0)
    def _():
        state_ref[...] = jnp.zeros(state_ref.shape, F32)

    for h in range(H_R):
        dq = slice(h * DK_R, (h + 1) * DK_R)
        dv = slice(h * DV_R, (h + 1) * DV_R)
        q = q_ref[:, dq]
        k = k_ref[:, dq]
        v = v_ref[:, dv]
        state = state_ref[h]
        inner = _dot_nt(q, k) * decay_ref[h]
        out = (_dot(inner.astype(BF16), v)
               + _dot((q.astype(F32) * xi_ref[h]).astype(BF16), state.astype(BF16)))
        state_ref[h] = (gc_ref[h][0:1, 0:1] * state
                        + _dot_tn((k.astype(F32) * zeta_ref[h]).astype(BF16), v))

        mu = jnp.mean(out, axis=-1, keepdims=True)
        d = out - mu
        var = jnp.mean(d * d, axis=-1, keepdims=True)
        yn = d * lax.rsqrt(var + RMS_EPS) * gn_ref[:, dv]
        gate = gate_ref[:, dv].astype(F32)
        y_ref[:, dv] = (gate * jax.nn.sigmoid(gate) * yn).astype(BF16)


def _retention(q, k, v, gate, gn_g):
    b, s, _ = q.shape
    c = RET_CHUNK
    log_gamma = jnp.log1p(-jnp.exp2(-5.0 - jnp.arange(H_R, dtype=F32)))
    j = jnp.arange(c, dtype=F32)
    diff = j[:, None] - j[None, :]
    decay = jnp.where(diff >= 0, jnp.exp(jnp.maximum(diff, 0.0) * log_gamma[:, None, None]), 0.0)
    xi = jnp.exp((j + 1.0) * log_gamma[:, None])[..., None]
    zeta = jnp.exp((c - 1.0 - j) * log_gamma[:, None])[..., None]
    gamma_c = jnp.broadcast_to(jnp.exp(c * log_gamma)[:, None, None], (H_R, 8, LANES))
    rows = lambda width: pl.BlockSpec((None, c, width), lambda bi, ci: (bi, ci, 0))
    return pl.pallas_call(
        _retention_kernel,
        grid=(b, s // c),
        in_specs=[_const_spec((H_R, c, c)), _const_spec((H_R, c, 1)), _const_spec((H_R, c, 1)),
                  _const_spec((H_R, 8, LANES)),
                  rows(H_R * DK_R), rows(H_R * DK_R), rows(H_R * DV_R), rows(H_R * DV_R),
                  _const_spec((1, H_R * DV_R))],
        out_specs=rows(H_R * DV_R),
        out_shape=jax.ShapeDtypeStruct((b, s, H_R * DV_R), BF16),
        scratch_shapes=[pltpu.VMEM((H_R, DK_R, DV_R), F32)],
        compiler_params=pltpu.CompilerParams(
            dimension_semantics=("parallel", "arbitrary"),
            vmem_limit_bytes=VMEM_LIMIT_BYTES),
        name="retention",
    )(decay, xi, zeta, gamma_c, q, k, v, gate, gn_g[None, :])


def kernel(x, positions, norm_mix_g, norm_mlp_g, w_in_a, w_out_a, w_in_b, ret_norm_g, w_out_b,
           w_mlp_up, w_mlp_down, final_norm_g):
    b, s, d = x.shape
    n = b * s
    x2 = x.reshape(n, d)
    pos2 = positions.reshape(n, 1)

    q, k, ksq, vt, qi, kilo, kihi, w = _dsa_in_proj(x2, pos2, norm_mix_g[0], w_in_a[0], b, s)
    r3 = lambda a: a.reshape(b, s, a.shape[-1])
    o = _dsa_attention(q, r3(qi), r3(w), r3(k), ksq, vt, r3(kilo), r3(kihi))
    h = _out_mlp(x2, o.reshape(n, -1), w_out_a[0], norm_mlp_g[0], w_mlp_up[0], w_mlp_down[0],
                 final_norm_g, final_norm=False)

    rq, rk, rv, rg = _ret_in_proj(h, pos2, norm_mix_g[1], w_in_b[0])
    y = _retention(r3(rq), r3(rk), r3(rv), r3(rg), ret_norm_g[0])
    out = _out_mlp(h, y.reshape(n, -1), w_out_b[0], norm_mlp_g[1], w_mlp_up[1], w_mlp_down[1],
                   final_norm_g, final_norm=True)
    return out.reshape(b, s, d)
```

```python
import functools
import math

import jax
import jax.numpy as jnp
import numpy as np
from jax import lax
from jax.experimental import pallas as pl
from jax.experimental.pallas import tpu as pltpu

D_MODEL = 1024
RMS_EPS = 1e-6
ROPE_THETA = 500000.0
H_A = 8
DH_A = D_MODEL // H_A
ROT_A = DH_A // 4
H_IDX = 8
D_IDX = 64
ROT_IDX = D_IDX // 4
TOPK_MAX = 256
H_R = 4
DK_R = D_MODEL // H_R
DV_R = 2 * DK_R
RET_THETA = 10000.0
D_FF = 4 * D_MODEL

LANES = 128
VMEM_LIMIT_BYTES = 56 * 1024 * 1024

ROW_TILE = 512
FF_TILE = 1024
Q_TILE = 256
K_TILE = 512
COUNT_ROWS = 64
COUNT16_ROWS = 128
RET_CHUNK = 256

NEG_BIG = -1e30
NORM_SLACK = 1.02
UNDERFLOW_GUARD = 2.0 ** -80
Q_SCALE = math.log2(math.e) / math.sqrt(DH_A)
INT_MIN = np.int32(-2 ** 31)
KEY_NEG_INF = np.int32(-2 ** 31 + 0x7FFFFF)
MANT_MASK = np.int32(0x7FFFFFFF)

BF16 = jnp.bfloat16
F32 = jnp.float32


def _dot(a, b):
    return jnp.dot(a, b, preferred_element_type=F32)


def _dot_nt(a, b):
    return lax.dot_general(a, b, (((1,), (1,)), ((), ())), preferred_element_type=F32)


def _dot_tn(a, b):
    return lax.dot_general(a, b, (((0,), (0,)), ((), ())), preferred_element_type=F32)


def _rms(x, g):
    ms = jnp.mean(x * x, axis=-1, keepdims=True)
    return x * lax.rsqrt(ms + RMS_EPS) * g


def _const_spec(shape):
    nd = len(shape)
    return pl.BlockSpec(shape, lambda *_: (0,) * nd, pipeline_mode=pl.Buffered(1))


def _dsa_in_kernel(x_ref, pos_ref, g_ref, wq_ref, wkv_ref, wqi_ref, wki_ref, ww_ref,
                   inv_a_ref, sgn_a_ref, inv_i_ref, sgn_i_ref,
                   q_ref, k_ref, ksq_ref, vt_ref, qi_ref, kilo_ref, kihi_ref, w_ref):
    hn = _rms(x_ref[...], g_ref[...]).astype(BF16)
    pos = pos_ref[...].astype(F32)
    lane = lax.broadcasted_iota(jnp.int32, (1, LANES), 1)

    ang_a = pos * inv_a_ref[...]
    cos_a = jnp.cos(ang_a)
    sin_a = jnp.sin(ang_a) * sgn_a_ref[...]
    first_a = lane < (ROT_A // 2)

    def rot_a(y):
        sw = jnp.where(first_a, pltpu.roll(y, LANES - ROT_A // 2, 1), pltpu.roll(y, ROT_A // 2, 1))
        return y * cos_a + sw * sin_a

    ang_i = pos * inv_i_ref[...]
    cos_i = jnp.cos(ang_i)
    sin_i = jnp.sin(ang_i) * sgn_i_ref[...]
    first_i = (lane & (D_IDX - 1)) < (ROT_IDX // 2)

    def rot_i(y):
        sw = jnp.where(first_i, pltpu.roll(y, LANES - ROT_IDX // 2, 1), pltpu.roll(y, ROT_IDX // 2, 1))
        return y * cos_i + sw * sin_i

    for c in range(H_A // 2):
        y = _dot(hn, wq_ref[:, c * 2 * DH_A:(c + 1) * 2 * DH_A])
        for t in range(2):
            h = 2 * c + t
            q_ref[2 * c + t] = (rot_a(y[:, t * DH_A:(t + 1) * DH_A]) * Q_SCALE).astype(BF16)

    kv = _dot(hn, wkv_ref[...])
    k_rot = rot_a(kv[:, :DH_A]).astype(BF16)
    k_ref[...] = k_rot
    k_f32 = k_rot.astype(F32)
    ksq_ref[...] = _dot_nt(jnp.ones((8, DH_A), BF16), (k_f32 * k_f32).astype(BF16))
    vt_ref[...] = kv[:, DH_A:].T.astype(BF16)

    for c in range(H_IDX * D_IDX // (2 * LANES)):
        y = _dot(hn, wqi_ref[:, c * 2 * LANES:(c + 1) * 2 * LANES])
        for t in range(2):
            j = 2 * c + t
            qi_ref[:, j * LANES:(j + 1) * LANES] = rot_i(y[:, t * LANES:(t + 1) * LANES]).astype(BF16)

    ki = _dot(hn, wki_ref[...])
    kilo_ref[...] = rot_i(ki[:, :LANES]).astype(BF16)
    kihi_ref[...] = rot_i(ki[:, LANES:]).astype(BF16)

    w_ref[...] = _dot(hn, ww_ref[...]) * (H_IDX ** -0.5 * D_IDX ** -0.5)


def _dsa_in_proj(x2, pos2, g, w_in, b, s):
    n = x2.shape[0]
    tm = ROW_TILE
    tiles_per_seq = s // tm
    o_q = H_A * DH_A
    o_k = o_q + DH_A
    o_v = o_k + DH_A
    o_qi = o_v + H_IDX * D_IDX
    o_ki = o_qi + D_IDX
    wq = w_in[:, :o_q].astype(BF16)
    wkv = w_in[:, o_q:o_v].astype(BF16)
    wqi = w_in[:, o_v:o_qi].astype(BF16)
    wki_cols = w_in[:, o_qi:o_ki]
    zeros = jnp.zeros_like(wki_cols)
    wki = jnp.concatenate([wki_cols, zeros, zeros, wki_cols], axis=1).astype(BF16)
    ww = jnp.pad(w_in[:, o_ki:], ((0, 0), (0, LANES - H_IDX))).astype(BF16)

    def pattern(rot, period):
        half = rot // 2
        inv = ROPE_THETA ** (-jnp.arange(half, dtype=F32) / half)
        inv_p = jnp.concatenate([inv, inv, jnp.zeros((period - rot,), F32)])
        sgn_p = jnp.concatenate([-jnp.ones((half,), F32), jnp.ones((half,), F32),
                                 jnp.zeros((period - rot,), F32)])
        reps = LANES // period
        return jnp.tile(inv_p, reps)[None, :], jnp.tile(sgn_p, reps)[None, :]

    inv_a, sgn_a = pattern(ROT_A, DH_A)
    inv_i, sgn_i = pattern(ROT_IDX, D_IDX)

    row = lambda width: pl.BlockSpec((tm, width), lambda i: (i, 0))
    rows_out = lambda width, dt: (row(width), jax.ShapeDtypeStruct((n, width), dt))
    vt_out = (pl.BlockSpec((None, DH_A, tm), lambda i: (i // tiles_per_seq, 0, i % tiles_per_seq)),
              jax.ShapeDtypeStruct((b, DH_A, s), BF16))
    q_out = (pl.BlockSpec((None, H_A, tm, DH_A),
                          lambda i: (i // tiles_per_seq, 0, i % tiles_per_seq, 0)),
             jax.ShapeDtypeStruct((b, H_A, s, DH_A), BF16))
    ksq_out = (pl.BlockSpec((None, 8, tm), lambda i: (i // tiles_per_seq, 0, i % tiles_per_seq)),
               jax.ShapeDtypeStruct((b, 8, s), F32))
    outs = [q_out, rows_out(DH_A, BF16), ksq_out, vt_out, rows_out(H_IDX * D_IDX, BF16),
            rows_out(LANES, BF16), rows_out(LANES, BF16), rows_out(LANES, F32)]
    return pl.pallas_call(
        _dsa_in_kernel,
        grid=(n // tm,),
        in_specs=[row(D_MODEL), row(1), _const_spec((1, D_MODEL)),
                  _const_spec(wq.shape), _const_spec(wkv.shape), _const_spec(wqi.shape),
                  _const_spec(wki.shape), _const_spec(ww.shape),
                  _const_spec((1, LANES)), _const_spec((1, LANES)),
                  _const_spec((1, LANES)), _const_spec((1, LANES))],
        out_specs=[o[0] for o in outs],
        out_shape=[o[1] for o in outs],
        compiler_params=pltpu.CompilerParams(dimension_semantics=("parallel",),
                                             vmem_limit_bytes=VMEM_LIMIT_BYTES),
        name="dsa_in_proj",
    )(x2, pos2, g[None, :], wq, wkv, wqi, wki, ww, inv_a, sgn_a, inv_i, sgn_i)


def _key_to_float(key):
    key = jnp.maximum(key, KEY_NEG_INF)
    bits = jnp.where(key >= 0, key, key ^ MANT_MASK)
    return lax.bitcast_convert_type(bits, F32)


def _sum_row_groups(x, n_acc=4, rows=8):
    groups = [x[g * rows:(g + 1) * rows, :] for g in range(x.shape[0] // rows)]
    accs = groups[:n_acc]
    for i, g in enumerate(groups[n_acc:]):
        accs[i % n_acc] = accs[i % n_acc] + g
    while len(accs) > 1:
        accs = [a + b for a, b in zip(accs[::2], accs[1::2])]
    return accs[0]


def _dsa_attn_kernel(q_ref, qi_ref, w_ref, k_ref, ksq_ref, vt_ref, kilo_ref, kihi_ref, o_ref,
                     sc_ref, sc16_ref, m_ref, l_ref, acc_ref, *, topk):
    tq = qi_ref.shape[0]
    q0 = pl.program_id(1) * tq
    n_ck = lax.div(q0 + (tq + K_TILE - 1), jnp.int32(K_TILE))
    query_t = q0 + lax.broadcasted_iota(jnp.int32, (1, tq), 1)
    key_i = lax.broadcasted_iota(jnp.int32, (K_TILE, 1), 0)
    w_t = w_ref[...].T

    def score_body(j, carry):
        off = pl.multiple_of(j * K_TILE, K_TILE)
        klo = kilo_ref[pl.ds(off, K_TILE), :]
        khi = kihi_ref[pl.ds(off, K_TILE), :]
        acc = jnp.zeros((K_TILE, tq), F32)
        for p in range(H_IDX // 2):
            a = qi_ref[:, p * LANES:(p + 1) * LANES]
            acc += jnp.maximum(_dot_nt(klo, a), 0.0) * w_t[2 * p:2 * p + 1, :]
            acc += jnp.maximum(_dot_nt(khi, a), 0.0) * w_t[2 * p + 1:2 * p + 2, :]
        sc = jnp.where(off + key_i <= query_t, acc, -jnp.inf)
        sc_ref[pl.ds(off, K_TILE), :] = sc
        sc16_ref[pl.ds(off, K_TILE), :] = sc.astype(BF16)
        return carry

    lax.fori_loop(0, n_ck, score_body, 0)

    def count_ge(thr):
        def body(j, part):
            off = pl.multiple_of(j * K_TILE, K_TILE)
            for r in range(K_TILE // COUNT_ROWS):
                sc = sc_ref[pl.ds(off + r * COUNT_ROWS, COUNT_ROWS), :]
                part = part + _sum_row_groups(jnp.where(sc >= thr, 1.0, 0.0))
            return part

        part = lax.fori_loop(0, n_ck, body, jnp.zeros((8, tq), F32))
        return jnp.sum(part, axis=0, keepdims=True)

    def count_ge16(thr16):
        one = jnp.ones((), BF16)
        zero = jnp.zeros((), BF16)

        def body(j, part):
            off = pl.multiple_of(j * K_TILE, K_TILE)
            p16 = None
            for r in range(K_TILE // COUNT16_ROWS):
                sc = sc16_ref[pl.ds(off + r * COUNT16_ROWS, COUNT16_ROWS), :]
                g = _sum_row_groups(jnp.where(sc >= thr16, one, zero), n_acc=2, rows=16)
                p16 = g if p16 is None else p16 + g
            return part + p16.astype(F32)

        part = lax.fori_loop(0, n_ck, body, jnp.zeros((16, tq), F32))
        return jnp.sum(part, axis=0, keepdims=True)

    def bf16_key(hi):
        key = lax.shift_left(hi, 16) ^ INT_MIN
        return jnp.where(key < 0, key | 0xFFFF, key)

    def coarse_body(it, hi):
        cand = hi | lax.shift_left(jnp.int32(1), 15 - it)
        cnt = count_ge16(_key_to_float(bf16_key(cand)).astype(BF16))
        return jnp.where(cnt >= float(topk), cand, hi)

    hi = lax.fori_loop(0, 16, coarse_body, jnp.zeros((1, tq), jnp.int32))
    base = bf16_key(hi) - 32768

    def search_body(it, carry):
        d, cnt_out = carry
        cand = d | lax.shift_left(jnp.int32(1), 16 - it)
        cnt = count_ge(_key_to_float(base + cand))
        ok = cnt >= float(topk)
        return (jnp.where(ok, cand, d), jnp.where(ok, cnt_out, cnt))

    d, cnt_gt = lax.fori_loop(0, 17, search_body,
                              (jnp.zeros((1, tq), jnp.int32), jnp.zeros((1, tq), F32)))
    key = base + d
    thr = _key_to_float(key)
    cnt_ge = count_ge(thr)
    thr_up = _key_to_float(key + 1)
    need = float(topk) - cnt_gt

    excess = jnp.where((cnt_ge - cnt_gt > need) & (key > KEY_NEG_INF), 1, 0)

    @pl.when(jnp.max(excess) > 0)
    def _():
        up_to = jnp.where(lax.broadcasted_iota(jnp.int32, (K_TILE, K_TILE), 1)
                          <= lax.broadcasted_iota(jnp.int32, (K_TILE, K_TILE), 0), 1.0, 0.0).astype(BF16)

        def body(j, seen):
            off = pl.multiple_of(j * K_TILE, K_TILE)
            sc = sc_ref[pl.ds(off, K_TILE), :]
            tie = (sc >= thr) & jnp.logical_not(sc >= thr_up)
            rank = _dot(up_to, jnp.where(tie, 1.0, 0.0).astype(BF16)) + seen
            sc_ref[pl.ds(off, K_TILE), :] = jnp.where(tie & (rank > need), -jnp.inf, sc)
            return rank[K_TILE - 1:K_TILE, :]

        lax.fori_loop(0, n_ck, body, jnp.zeros((1, tq), F32))

    def key_tile(j):
        off = pl.multiple_of(j * K_TILE, K_TILE)
        sc = sc_ref[pl.ds(off, K_TILE), :]
        bias = jnp.where((sc >= thr) & (off + key_i <= query_t), 0.0, NEG_BIG)
        q_all = q_ref[...].reshape(H_A * tq, DH_A)
        lg_all = _dot_nt(k_ref[pl.ds(off, K_TILE), :], q_all)
        return bias, lg_all, vt_ref[:, pl.ds(off, K_TILE)]

    q_f32 = q_ref[...].reshape(H_A * tq, DH_A).astype(F32)
    q_sq = _dot_nt(jnp.ones((8, DH_A), BF16), (q_f32 * q_f32).astype(BF16))[0:1, :]
    k_sq_max = jnp.max(jnp.max(ksq_ref[...], axis=1, keepdims=True), axis=0, keepdims=True)
    shift = jnp.sqrt(q_sq * k_sq_max) * NORM_SLACK
    l_ref[...] = jnp.zeros(l_ref.shape, F32)
    acc_ref[...] = jnp.zeros(acc_ref.shape, F32)

    def bounded_body(j, carry):
        bias, lg_all, vtc = key_tile(j)
        ps, sums = [], []
        for h in range(H_A):
            cols = slice(h * tq, (h + 1) * tq)
            p = jnp.exp2(lg_all[:, cols] - shift[:, cols] + bias)
            sums.append(jnp.sum(p, axis=0, keepdims=True))
            ps.append(p.astype(BF16))
        l_ref[...] += jnp.concatenate(sums, axis=1)
        acc_ref[...] += _dot(vtc, jnp.concatenate(ps, axis=1))
        return carry

    lax.fori_loop(0, n_ck, bounded_body, 0)

    @pl.when(jnp.min(l_ref[...]) < UNDERFLOW_GUARD)
    def _():
        m_ref[...] = jnp.full(m_ref.shape, NEG_BIG, F32)
        l_ref[...] = jnp.zeros(l_ref.shape, F32)
        acc_ref[...] = jnp.zeros(acc_ref.shape, F32)

        def online_body(j, carry):
            bias, lg_all, vtc = key_tile(j)
            m_prev = m_ref[...]
            l_prev = l_ref[...]
            ps, m_news, l_news = [], [], []
            for h in range(H_A):
                cols = slice(h * tq, (h + 1) * tq)
                lg = lg_all[:, cols] + bias
                m_new = jnp.maximum(m_prev[:, cols], jnp.max(lg, axis=0, keepdims=True))
                p = jnp.exp2(lg - m_new)
                l_news.append(jnp.exp2(m_prev[:, cols] - m_new) * l_prev[:, cols]
                              + jnp.sum(p, axis=0, keepdims=True))
                m_news.append(m_new)
                ps.append(p.astype(BF16))
            m_new = jnp.concatenate(m_news, axis=1)
            l_ref[...] = jnp.concatenate(l_news, axis=1)
            acc_ref[...] = (jnp.exp2(m_prev - m_new) * acc_ref[...]
                            + _dot(vtc, jnp.concatenate(ps, axis=1)))
            m_ref[...] = m_new
            return carry

        lax.fori_loop(0, n_ck, online_body, 0)

    out_t = acc_ref[...] / l_ref[...]
    for h in range(H_A):
        o_ref[:, h * DH_A:(h + 1) * DH_A] = out_t[:, h * tq:(h + 1) * tq].T.astype(BF16)


def _dsa_attention(q, qi, w, k, ksq, vt, kilo, kihi):
    b, s, _ = k.shape
    tq = Q_TILE
    topk = min(TOPK_MAX, s // 4)
    qt = lambda width: pl.BlockSpec((None, tq, width), lambda bi, qi_: (bi, qi_, 0))
    full = lambda width: pl.BlockSpec((None, s, width), lambda bi, qi_: (bi, 0, 0))
    return pl.pallas_call(
        functools.partial(_dsa_attn_kernel, topk=topk),
        grid=(b, s // tq),
        in_specs=[pl.BlockSpec((None, H_A, tq, DH_A), lambda bi, qi_: (bi, 0, qi_, 0)),
                  qt(H_IDX * D_IDX), qt(LANES), full(DH_A),
                  pl.BlockSpec((None, 8, s), lambda bi, qi_: (bi, 0, 0)),
                  pl.BlockSpec((None, DH_A, s), lambda bi, qi_: (bi, 0, 0)),
                  full(LANES), full(LANES)],
        out_specs=qt(H_A * DH_A),
        out_shape=jax.ShapeDtypeStruct((b, s, H_A * DH_A), BF16),
        scratch_shapes=[pltpu.VMEM((s, tq), F32),
                        pltpu.VMEM((s, tq), BF16),
                        pltpu.VMEM((1, H_A * tq), F32),
                        pltpu.VMEM((1, H_A * tq), F32),
                        pltpu.VMEM((DH_A, H_A * tq), F32)],
        compiler_params=pltpu.CompilerParams(dimension_semantics=("parallel", "parallel"),
                                             vmem_limit_bytes=VMEM_LIMIT_BYTES),
        name="dsa_attention",
    )(q, qi, w, k, ksq, vt, kilo, kihi)


def _out_mlp_kernel(x_ref, y_ref, wo_ref, g_ref, wup_ref, wdn_ref, gf_ref, o_ref, *, final_norm):
    h = x_ref[...] + _dot(y_ref[...], wo_ref[...])
    hn = _rms(h, g_ref[...]).astype(BF16)
    acc = jnp.zeros(h.shape, F32)
    for c in range(D_FF // FF_TILE):
        u = jnp.maximum(_dot(hn, wup_ref[:, c * FF_TILE:(c + 1) * FF_TILE]), 0.0)
        acc += _dot((u * u).astype(BF16), wdn_ref[c * FF_TILE:(c + 1) * FF_TILE, :])
    h = h + acc
    if final_norm:
        h = _rms(h, gf_ref[...])
    o_ref[...] = h


def _out_mlp(x2, y2, w_out, g_mlp, w_up, w_down, g_final, final_norm):
    n = x2.shape[0]
    tm = ROW_TILE
    kin = y2.shape[1]
    row = lambda width: pl.BlockSpec((tm, width), lambda i: (i, 0))
    return pl.pallas_call(
        functools.partial(_out_mlp_kernel, final_norm=final_norm),
        grid=(n // tm,),
        in_specs=[row(D_MODEL), row(kin), _const_spec((kin, D_MODEL)), _const_spec((1, D_MODEL)),
                  _const_spec((D_MODEL, D_FF)), _const_spec((D_FF, D_MODEL)),
                  _const_spec((1, D_MODEL))],
        out_specs=row(D_MODEL),
        out_shape=jax.ShapeDtypeStruct((n, D_MODEL), F32),
        compiler_params=pltpu.CompilerParams(dimension_semantics=("parallel",),
                                             vmem_limit_bytes=VMEM_LIMIT_BYTES),
        name="out_proj_mlp_final" if final_norm else "out_proj_mlp",
    )(x2, y2, w_out.astype(BF16), g_mlp[None, :], w_up.astype(BF16), w_down.astype(BF16),
      g_final[None, :])


def _ret_in_kernel(x_ref, pos_ref, g_ref, wq_ref, wk_ref, wv_ref, wg_ref, inv_ref,
                   q_ref, k_ref, v_ref, gate_ref):
    hn = _rms(x_ref[...], g_ref[...]).astype(BF16)
    ang = pos_ref[...].astype(F32) * inv_ref[...]
    cos = jnp.cos(ang)
    sin = jnp.sin(ang)
    half = DK_R // 2

    def rot_store(w_ref, out_ref, mul):
        for h in range(H_R):
            y = _dot(hn, w_ref[:, h * DK_R:(h + 1) * DK_R])
            x1 = y[:, :half]
            x2 = y[:, half:]
            out_ref[:, h * DK_R:h * DK_R + half] = ((x1 * cos - x2 * sin) * mul).astype(BF16)
            out_ref[:, h * DK_R + half:(h + 1) * DK_R] = ((x2 * cos + x1 * sin) * mul).astype(BF16)

    rot_store(wq_ref, q_ref, 1.0)
    rot_store(wk_ref, k_ref, DK_R ** -0.5)
    for c in range(H_R):
        v_ref[:, c * DV_R:(c + 1) * DV_R] = _dot(hn, wv_ref[:, c * DV_R:(c + 1) * DV_R]).astype(BF16)
        gate_ref[:, c * DV_R:(c + 1) * DV_R] = _dot(hn, wg_ref[:, c * DV_R:(c + 1) * DV_R]).astype(BF16)


def _ret_in_proj(x2, pos2, g, w_in):
    n = x2.shape[0]
    tm = ROW_TILE
    dq = H_R * DK_R
    dv = H_R * DV_R
    wq = w_in[:, :dq].astype(BF16)
    wk = w_in[:, dq:2 * dq].astype(BF16)
    wv = w_in[:, 2 * dq:2 * dq + dv].astype(BF16)
    wg = w_in[:, 2 * dq + dv:].astype(BF16)
    half = DK_R // 2
    inv = (RET_THETA ** (-jnp.arange(half, dtype=F32) / half))[None, :]
    row = lambda width: pl.BlockSpec((tm, width), lambda i: (i, 0))
    outs = [(n, dq), (n, dq), (n, dv), (n, dv)]
    return pl.pallas_call(
        _ret_in_kernel,
        grid=(n // tm,),
        in_specs=[row(D_MODEL), row(1), _const_spec((1, D_MODEL)),
                  _const_spec(wq.shape), _const_spec(wk.shape), _const_spec(wv.shape),
                  _const_spec(wg.shape), _const_spec((1, half))],
        out_specs=[row(s[1]) for s in outs],
        out_shape=[jax.ShapeDtypeStruct(s, BF16) for s in outs],
        compiler_params=pltpu.CompilerParams(dimension_semantics=("parallel",),
                                             vmem_limit_bytes=VMEM_LIMIT_BYTES),
        name="ret_in_proj",
    )(x2, pos2, g[None, :], wq, wk, wv, wg, inv)


def _retention_kernel(decay_ref, xi_ref, zeta_ref, gc_ref, q_ref, k_ref, v_ref, gate_ref, gn_ref,
                      y_ref, state_ref):
    @pl.when(pl.program_id(1) == 0)
    def _():
        state_ref[...] = jnp.zeros(state_ref.shape, F32)

    for h in range(H_R):
        dq = slice(h * DK_R, (h + 1) * DK_R)
        dv = slice(h * DV_R, (h + 1) * DV_R)
        q = q_ref[:, dq]
        k = k_ref[:, dq]
        v = v_ref[:, dv]
        state = state_ref[h]
        inner = _dot_nt(q, k) * decay_ref[h]
        out = (_dot(inner.astype(BF16), v)
               + _dot((q.astype(F32) * xi_ref[h]).astype(BF16), state.astype(BF16)))
        state_ref[h] = (gc_ref[h][0:1, 0:1] * state
                        + _dot_tn((k.astype(F32) * zeta_ref[h]).astype(BF16), v))

        mu = jnp.mean(out, axis=-1, keepdims=True)
        d = out - mu
        var = jnp.mean(d * d, axis=-1, keepdims=True)
        yn = d * lax.rsqrt(var + RMS_EPS) * gn_ref[:, dv]
        gate = gate_ref[:, dv].astype(F32)
        y_ref[:, dv] = (gate * jax.nn.sigmoid(gate) * yn).astype(BF16)


def _retention(q, k, v, gate, gn_g):
    b, s, _ = q.shape
    c = RET_CHUNK
    log_gamma = jnp.log1p(-jnp.exp2(-5.0 - jnp.arange(H_R, dtype=F32)))
    j = jnp.arange(c, dtype=F32)
    diff = j[:, None] - j[None, :]
    decay = jnp.where(diff >= 0, jnp.exp(jnp.maximum(diff, 0.0) * log_gamma[:, None, None]), 0.0)
    xi = jnp.exp((j + 1.0) * log_gamma[:, None])[..., None]
    zeta = jnp.exp((c - 1.0 - j) * log_gamma[:, None])[..., None]
    gamma_c = jnp.broadcast_to(jnp.exp(c * log_gamma)[:, None, None], (H_R, 8, LANES))
    rows = lambda width: pl.BlockSpec((None, c, width), lambda bi, ci: (bi, ci, 0))
    return pl.pallas_call(
        _retention_kernel,
        grid=(b, s // c),
        in_specs=[_const_spec((H_R, c, c)), _const_spec((H_R, c, 1)), _const_spec((H_R, c, 1)),
                  _const_spec((H_R, 8, LANES)),
                  rows(H_R * DK_R), rows(H_R * DK_R), rows(H_R * DV_R), rows(H_R * DV_R),
                  _const_spec((1, H_R * DV_R))],
        out_specs=rows(H_R * DV_R),
        out_shape=jax.ShapeDtypeStruct((b, s, H_R * DV_R), BF16),
        scratch_shapes=[pltpu.VMEM((H_R, DK_R, DV_R), F32)],
        compiler_params=pltpu.CompilerParams(
            dimension_semantics=("parallel", "arbitrary"),
            vmem_limit_bytes=VMEM_LIMIT_BYTES),
        name="retention",
    )(decay, xi, zeta, gamma_c, q, k, v, gate, gn_g[None, :])


def kernel(x, positions, norm_mix_g, norm_mlp_g, w_in_a, w_out_a, w_in_b, ret_norm_g, w_out_b,
           w_mlp_up, w_mlp_down, final_norm_g):
    b, s, d = x.shape
    n = b * s
    x2 = x.reshape(n, d)
    pos2 = positions.reshape(n, 1)

    q, k, ksq, vt, qi, kilo, kihi, w = _dsa_in_proj(x2, pos2, norm_mix_g[0], w_in_a[0], b, s)
    r3 = lambda a: a.reshape(b, s, a.shape[-1])
    o = _dsa_attention(q, r3(qi), r3(w), r3(k), ksq, vt, r3(kilo), r3(kihi))
    h = _out_mlp(x2, o.reshape(n, -1), w_out_a[0], norm_mlp_g[0], w_mlp_up[0], w_mlp_down[0],
                 final_norm_g, final_norm=False)

    rq, rk, rv, rg = _ret_in_proj(h, pos2, norm_mix_g[1], w_in_b[0])
    y = _retention(r3(rq), r3(rk), r3(rv), r3(rg), ret_norm_g[0])
    out = _out_mlp(h, y.reshape(n, -1), w_out_b[0], norm_mlp_g[1], w_mlp_up[1], w_mlp_down[1],
                   final_norm_g, final_norm=True)
    return out.reshape(b, s, d)
```

```python
import functools
import math

import jax
import jax.numpy as jnp
import numpy as np
from jax import lax
from jax.experimental import pallas as pl
from jax.experimental.pallas import tpu as pltpu

D_MODEL = 1024
RMS_EPS = 1e-6
ROPE_THETA = 500000.0
H_A = 8
DH_A = D_MODEL // H_A
ROT_A = DH_A // 4
H_IDX = 8
D_IDX = 64
ROT_IDX = D_IDX // 4
TOPK_MAX = 256
H_R = 4
DK_R = D_MODEL // H_R
DV_R = 2 * DK_R
RET_THETA = 10000.0
D_FF = 4 * D_MODEL

LANES = 128
HALF_TILE = LANES // 2
assert DH_A == LANES and 2 * D_IDX == LANES and ROT_IDX == ROT_A // 2 and ROT_A <= HALF_TILE
VMEM_LIMIT_BYTES = 56 * 1024 * 1024

ROW_TILE = 512
FF_TILE = 1024
Q_TILE = 256
K_TILE = 512
COUNT_ROWS = 64
COUNT16_ROWS = 128
RET_CHUNK = 256

NEG_BIG = -1e30
NORM_SLACK = 1.02
UNDERFLOW_GUARD = 2.0 ** -80
Q_SCALE = math.log2(math.e) / math.sqrt(DH_A)
INT_MIN = np.int32(-2 ** 31)
KEY_NEG_INF = np.int32(-2 ** 31 + 0x7FFFFF)
MANT_MASK = np.int32(0x7FFFFFFF)

BF16 = jnp.bfloat16
F32 = jnp.float32


def _dot(a, b):
    return jnp.dot(a, b, preferred_element_type=F32)


def _dot_nt(a, b):
    return lax.dot_general(a, b, (((1,), (1,)), ((), ())), preferred_element_type=F32)


def _dot_tn(a, b):
    return lax.dot_general(a, b, (((0,), (0,)), ((), ())), preferred_element_type=F32)


def _rms(x, g):
    ms = jnp.mean(x * x, axis=-1, keepdims=True)
    return x * lax.rsqrt(ms + RMS_EPS) * g


def _const_spec(shape):
    nd = len(shape)
    return pl.BlockSpec(shape, lambda *_: (0,) * nd, pipeline_mode=pl.Buffered(1))


def _dsa_in_kernel(x_ref, pos_ref, g_ref, wq_ref, wkv_ref, wqi_ref, wki_ref, ww_ref,
                   inv_ref, sgn_ref,
                   q_ref, k_ref, ksq_ref, vt_ref, qi_ref, kilo_ref, kihi_ref, w_ref):
    hn = _rms(x_ref[...], g_ref[...]).astype(BF16)
    pos = pos_ref[...].astype(F32)

    sgn = sgn_ref[...]
    rot_lane = (lax.broadcasted_iota(jnp.int32, (1, LANES), 1) & (HALF_TILE - 1)) < ROT_A // 2
    ang = pos * inv_ref[...]
    cos_t = jnp.cos(ang)
    sin_t = jnp.sin(ang)
    cos_a = jnp.where(rot_lane, cos_t, 1.0)
    sin_a = sin_t * sgn
    cos_i = jnp.where(rot_lane, pltpu.roll(cos_t, LANES - ROT_A // 2, 1), 1.0)
    sin_i = pltpu.roll(sin_t, LANES - ROT_A // 2, 1) * sgn

    def rot_a(y):
        return y * cos_a + pltpu.roll(y, HALF_TILE, 1) * sin_a

    def rot_i(y):
        return y * cos_i + pltpu.roll(y, HALF_TILE, 1) * sin_i

    for c in range(H_A // 2):
        y = _dot(hn, wq_ref[:, c * 2 * DH_A:(c + 1) * 2 * DH_A])
        for t in range(2):
            h = 2 * c + t
            q_ref[2 * c + t] = (rot_a(y[:, t * DH_A:(t + 1) * DH_A]) * Q_SCALE).astype(BF16)

    kv = _dot(hn, wkv_ref[...])
    k_rot = rot_a(kv[:, :DH_A]).astype(BF16)
    k_ref[...] = k_rot
    k_f32 = k_rot.astype(F32)
    ksq_ref[...] = _dot_nt(jnp.ones((8, DH_A), BF16), (k_f32 * k_f32).astype(BF16))
    vt_ref[...] = kv[:, DH_A:].T.astype(BF16)

    for c in range(H_IDX * D_IDX // (2 * LANES)):
        y = _dot(hn, wqi_ref[:, c * 2 * LANES:(c + 1) * 2 * LANES])
        for t in range(2):
            j = 2 * c + t
            qi_ref[:, j * LANES:(j + 1) * LANES] = rot_i(y[:, t * LANES:(t + 1) * LANES]).astype(BF16)

    ki = _dot(hn, wki_ref[...])
    kilo_ref[...] = rot_i(ki[:, :LANES]).astype(BF16)
    kihi_ref[...] = rot_i(ki[:, LANES:]).astype(BF16)

    w_ref[...] = _dot(hn, ww_ref[...]) * (H_IDX ** -0.5 * D_IDX ** -0.5)


def _dsa_in_proj(x2, pos2, g, w_in, b, s):
    n = x2.shape[0]
    tm = ROW_TILE
    tiles_per_seq = s // tm
    o_q = H_A * DH_A
    o_k = o_q + DH_A
    o_v = o_k + DH_A
    o_qi = o_v + H_IDX * D_IDX
    o_ki = o_qi + D_IDX
    ha = ROT_A // 2
    cat = lambda parts: jnp.concatenate(parts, axis=-1)

    def head_order(w):
        return cat([w[..., :ha], w[..., ROT_A:ROT_A + HALF_TILE - ha],
                    w[..., ha:ROT_A], w[..., ROT_A + HALF_TILE - ha:]])

    wq = head_order(w_in[:, :o_q].reshape(D_MODEL, H_A, DH_A)).reshape(D_MODEL, o_q).astype(BF16)
    wkv = cat([head_order(w_in[:, o_q:o_k]), w_in[:, o_k:o_v]]).astype(BF16)

    hi_ = ROT_IDX // 2

    def pair_order(a, b):
        return cat([a[..., :hi_], b[..., :hi_], a[..., ROT_IDX:],
                    a[..., hi_:ROT_IDX], b[..., hi_:ROT_IDX], b[..., ROT_IDX:]])

    wqi_heads = w_in[:, o_v:o_qi].reshape(D_MODEL, H_IDX // 2, 2, D_IDX)
    wqi = (pair_order(wqi_heads[:, :, 0], wqi_heads[:, :, 1])
           .reshape(D_MODEL, H_IDX * D_IDX).astype(BF16))
    wki_cols = w_in[:, o_qi:o_ki]
    none = jnp.zeros_like(wki_cols)
    wki = cat([pair_order(wki_cols, none), pair_order(none, wki_cols)]).astype(BF16)
    ww = jnp.pad(w_in[:, o_ki:], ((0, 0), (0, LANES - H_IDX))).astype(BF16)

    inv_of = lambda rot: ROPE_THETA ** (-jnp.arange(rot // 2, dtype=F32) / (rot // 2))
    half_row = jnp.concatenate([inv_of(ROT_A), inv_of(ROT_IDX), inv_of(ROT_IDX),
                                jnp.zeros((HALF_TILE - ROT_A,), F32)])
    inv = jnp.concatenate([half_row, half_row])[None, :]
    n_rot = 2 * hi_
    sgn = jnp.concatenate([-jnp.ones((n_rot,), F32), jnp.zeros((HALF_TILE - n_rot,), F32),
                           jnp.ones((n_rot,), F32), jnp.zeros((HALF_TILE - n_rot,), F32)])[None, :]

    row = lambda width: pl.BlockSpec((tm, width), lambda i: (i, 0))
    rows_out = lambda width, dt: (row(width), jax.ShapeDtypeStruct((n, width), dt))
    vt_out = (pl.BlockSpec((None, DH_A, tm), lambda i: (i // tiles_per_seq, 0, i % tiles_per_seq)),
              jax.ShapeDtypeStruct((b, DH_A, s), BF16))
    q_out = (pl.BlockSpec((None, H_A, tm, DH_A),
                          lambda i: (i // tiles_per_seq, 0, i % tiles_per_seq, 0)),
             jax.ShapeDtypeStruct((b, H_A, s, DH_A), BF16))
    ksq_out = (pl.BlockSpec((None, 8, tm), lambda i: (i // tiles_per_seq, 0, i % tiles_per_seq)),
               jax.ShapeDtypeStruct((b, 8, s), F32))
    outs = [q_out, rows_out(DH_A, BF16), ksq_out, vt_out, rows_out(H_IDX * D_IDX, BF16),
            rows_out(LANES, BF16), rows_out(LANES, BF16), rows_out(LANES, F32)]
    return pl.pallas_call(
        _dsa_in_kernel,
        grid=(n // tm,),
        in_specs=[row(D_MODEL), row(1), _const_spec((1, D_MODEL)),
                  _const_spec(wq.shape), _const_spec(wkv.shape), _const_spec(wqi.shape),
                  _const_spec(wki.shape), _const_spec(ww.shape),
                  _const_spec((1, LANES)), _const_spec((1, LANES))],
        out_specs=[o[0] for o in outs],
        out_shape=[o[1] for o in outs],
        compiler_params=pltpu.CompilerParams(dimension_semantics=("parallel",),
                                             vmem_limit_bytes=VMEM_LIMIT_BYTES),
        name="dsa_in_proj",
    )(x2, pos2, g[None, :], wq, wkv, wqi, wki, ww, inv, sgn)


def _key_to_float(key):
    key = jnp.maximum(key, KEY_NEG_INF)
    bits = jnp.where(key >= 0, key, key ^ MANT_MASK)
    return lax.bitcast_convert_type(bits, F32)


def _sum_row_groups(x, n_acc=4, rows=8):
    groups = [x[g * rows:(g + 1) * rows, :] for g in range(x.shape[0] // rows)]
    accs = groups[:n_acc]
    for i, g in enumerate(groups[n_acc:]):
        accs[i % n_acc] = accs[i % n_acc] + g
    while len(accs) > 1:
        accs = [a + b for a, b in zip(accs[::2], accs[1::2])]
    return accs[0]


def _dsa_attn_kernel(q_ref, qi_ref, w_ref, k_ref, ksq_ref, vt_ref, kilo_ref, kihi_ref, o_ref,
                     sc_ref, sc16_ref, m_ref, l_ref, acc_ref, *, topk):
    tq = qi_ref.shape[0]
    q0 = pl.program_id(1) * tq
    n_ck = lax.div(q0 + (tq + K_TILE - 1), jnp.int32(K_TILE))
    query_t = q0 + lax.broadcasted_iota(jnp.int32, (1, tq), 1)
    key_i = lax.broadcasted_iota(jnp.int32, (K_TILE, 1), 0)
    w_t = w_ref[...].T

    def score_body(j, carry):
        off = pl.multiple_of(j * K_TILE, K_TILE)
        klo = kilo_ref[pl.ds(off, K_TILE), :]
        khi = kihi_ref[pl.ds(off, K_TILE), :]
        acc = jnp.zeros((K_TILE, tq), F32)
        for p in range(H_IDX // 2):
            a = qi_ref[:, p * LANES:(p + 1) * LANES]
            acc += jnp.maximum(_dot_nt(klo, a), 0.0) * w_t[2 * p:2 * p + 1, :]
            acc += jnp.maximum(_dot_nt(khi, a), 0.0) * w_t[2 * p + 1:2 * p + 2, :]
        sc = jnp.where(off + key_i <= query_t, acc, -jnp.inf)
        sc_ref[pl.ds(off, K_TILE), :] = sc
        sc16_ref[pl.ds(off, K_TILE), :] = sc.astype(BF16)
        return carry

    lax.fori_loop(0, n_ck, score_body, 0)

    def count_ge(thr):
        def body(j, part):
            off = pl.multiple_of(j * K_TILE, K_TILE)
            for r in range(K_TILE // COUNT_ROWS):
                sc = sc_ref[pl.ds(off + r * COUNT_ROWS, COUNT_ROWS), :]
                part = part + _sum_row_groups(jnp.where(sc >= thr, 1.0, 0.0))
            return part

        part = lax.fori_loop(0, n_ck, body, jnp.zeros((8, tq), F32))
        return jnp.sum(part, axis=0, keepdims=True)

    def count_ge16(thr16):
        one = jnp.ones((), BF16)
        zero = jnp.zeros((), BF16)

        def body(j, part):
            off = pl.multiple_of(j * K_TILE, K_TILE)
            p16 = None
            for r in range(K_TILE // COUNT16_ROWS):
                sc = sc16_ref[pl.ds(off + r * COUNT16_ROWS, COUNT16_ROWS), :]
                g = _sum_row_groups(jnp.where(sc >= thr16, one, zero), n_acc=2, rows=16)
                p16 = g if p16 is None else p16 + g
            return part + p16.astype(F32)

        part = lax.fori_loop(0, n_ck, body, jnp.zeros((16, tq), F32))
        return jnp.sum(part, axis=0, keepdims=True)

    def bf16_key(hi):
        key = lax.shift_left(hi, 16) ^ INT_MIN
        return jnp.where(key < 0, key | 0xFFFF, key)

    def coarse_body(it, hi):
        cand = hi | lax.shift_left(jnp.int32(1), 15 - it)
        cnt = count_ge16(_key_to_float(bf16_key(cand)).astype(BF16))
        return jnp.where(cnt >= float(topk), cand, hi)

    hi = lax.fori_loop(0, 16, coarse_body, jnp.zeros((1, tq), jnp.int32))
    base = bf16_key(hi) - 32768

    def search_body(it, carry):
        d, cnt_out = carry
        cand = d | lax.shift_left(jnp.int32(1), 16 - it)
        cnt = count_ge(_key_to_float(base + cand))
        ok = cnt >= float(topk)
        return (jnp.where(ok, cand, d), jnp.where(ok, cnt_out, cnt))

    d, cnt_gt = lax.fori_loop(0, 17, search_body,
                              (jnp.zeros((1, tq), jnp.int32), jnp.zeros((1, tq), F32)))
    key = base + d
    thr = _key_to_float(key)
    cnt_ge = count_ge(thr)
    thr_up = _key_to_float(key + 1)
    need = float(topk) - cnt_gt

    excess = jnp.where((cnt_ge - cnt_gt > need) & (key > KEY_NEG_INF), 1, 0)

    @pl.when(jnp.max(excess) > 0)
    def _():
        up_to = jnp.where(lax.broadcasted_iota(jnp.int32, (K_TILE, K_TILE), 1)
                          <= lax.broadcasted_iota(jnp.int32, (K_TILE, K_TILE), 0), 1.0, 0.0).astype(BF16)

        def body(j, seen):
            off = pl.multiple_of(j * K_TILE, K_TILE)
            sc = sc_ref[pl.ds(off, K_TILE), :]
            tie = (sc >= thr) & jnp.logical_not(sc >= thr_up)
            rank = _dot(up_to, jnp.where(tie, 1.0, 0.0).astype(BF16)) + seen
            sc_ref[pl.ds(off, K_TILE), :] = jnp.where(tie & (rank > need), -jnp.inf, sc)
            return rank[K_TILE - 1:K_TILE, :]

        lax.fori_loop(0, n_ck, body, jnp.zeros((1, tq), F32))

    def key_tile(j):
        off = pl.multiple_of(j * K_TILE, K_TILE)
        sc = sc_ref[pl.ds(off, K_TILE), :]
        bias = jnp.where((sc >= thr) & (off + key_i <= query_t), 0.0, NEG_BIG)
        q_all = q_ref[...].reshape(H_A * tq, DH_A)
        lg_all = _dot_nt(k_ref[pl.ds(off, K_TILE), :], q_all)
        return bias, lg_all, vt_ref[:, pl.ds(off, K_TILE)]

    q_f32 = q_ref[...].reshape(H_A * tq, DH_A).astype(F32)
    q_sq = _dot_nt(jnp.ones((8, DH_A), BF16), (q_f32 * q_f32).astype(BF16))[0:1, :]
    k_sq_max = jnp.max(jnp.max(ksq_ref[...], axis=1, keepdims=True), axis=0, keepdims=True)
    shift = jnp.sqrt(q_sq * k_sq_max) * NORM_SLACK
    l_ref[...] = jnp.zeros(l_ref.shape, F32)
    acc_ref[...] = jnp.zeros(acc_ref.shape, F32)

    def bounded_body(j, carry):
        bias, lg_all, vtc = key_tile(j)
        ps, sums = [], []
        for h in range(H_A):
            cols = slice(h * tq, (h + 1) * tq)
            p = jnp.exp2(lg_all[:, cols] - shift[:, cols] + bias)
            sums.append(jnp.sum(p, axis=0, keepdims=True))
            ps.append(p.astype(BF16))
        l_ref[...] += jnp.concatenate(sums, axis=1)
        acc_ref[...] += _dot(vtc, jnp.concatenate(ps, axis=1))
        return carry

    lax.fori_loop(0, n_ck, bounded_body, 0)

    @pl.when(jnp.min(l_ref[...]) < UNDERFLOW_GUARD)
    def _():
        m_ref[...] = jnp.full(m_ref.shape, NEG_BIG, F32)
        l_ref[...] = jnp.zeros(l_ref.shape, F32)
        acc_ref[...] = jnp.zeros(acc_ref.shape, F32)

        def online_body(j, carry):
            bias, lg_all, vtc = key_tile(j)
            m_prev = m_ref[...]
            l_prev = l_ref[...]
            ps, m_news, l_news = [], [], []
            for h in range(H_A):
                cols = slice(h * tq, (h + 1) * tq)
                lg = lg_all[:, cols] + bias
                m_new = jnp.maximum(m_prev[:, cols], jnp.max(lg, axis=0, keepdims=True))
                p = jnp.exp2(lg - m_new)
                l_news.append(jnp.exp2(m_prev[:, cols] - m_new) * l_prev[:, cols]
                              + jnp.sum(p, axis=0, keepdims=True))
                m_news.append(m_new)
                ps.append(p.astype(BF16))
            m_new = jnp.concatenate(m_news, axis=1)
            l_ref[...] = jnp.concatenate(l_news, axis=1)
            acc_ref[...] = (jnp.exp2(m_prev - m_new) * acc_ref[...]
                            + _dot(vtc, jnp.concatenate(ps, axis=1)))
            m_ref[...] = m_new
            return carry

        lax.fori_loop(0, n_ck, online_body, 0)

    out_t = acc_ref[...] / l_ref[...]
    for h in range(H_A):
        o_ref[:, h * DH_A:(h + 1) * DH_A] = out_t[:, h * tq:(h + 1) * tq].T.astype(BF16)


def _dsa_attention(q, qi, w, k, ksq, vt, kilo, kihi):
    b, s, _ = k.shape
    tq = Q_TILE
    topk = min(TOPK_MAX, s // 4)
    qt = lambda width: pl.BlockSpec((None, tq, width), lambda bi, qi_: (bi, qi_, 0))
    full = lambda width: pl.BlockSpec((None, s, width), lambda bi, qi_: (bi, 0, 0))
    return pl.pallas_call(
        functools.partial(_dsa_attn_kernel, topk=topk),
        grid=(b, s // tq),
        in_specs=[pl.BlockSpec((None, H_A, tq, DH_A), lambda bi, qi_: (bi, 0, qi_, 0)),
                  qt(H_IDX * D_IDX), qt(LANES), full(DH_A),
                  pl.BlockSpec((None, 8, s), lambda bi, qi_: (bi, 0, 0)),
                  pl.BlockSpec((None, DH_A, s), lambda bi, qi_: (bi, 0, 0)),
                  full(LANES), full(LANES)],
        out_specs=qt(H_A * DH_A),
        out_shape=jax.ShapeDtypeStruct((b, s, H_A * DH_A), BF16),
        scratch_shapes=[pltpu.VMEM((s, tq), F32),
                        pltpu.VMEM((s, tq), BF16),
                        pltpu.VMEM((1, H_A * tq), F32),
                        pltpu.VMEM((1, H_A * tq), F32),
                        pltpu.VMEM((DH_A, H_A * tq), F32)],
        compiler_params=pltpu.CompilerParams(dimension_semantics=("parallel", "parallel"),
                                             vmem_limit_bytes=VMEM_LIMIT_BYTES),
        name="dsa_attention",
    )(q, qi, w, k, ksq, vt, kilo, kihi)


def _out_mlp_kernel(x_ref, y_ref, wo_ref, g_ref, wup_ref, wdn_ref, gf_ref, o_ref, *, final_norm):
    h = x_ref[...] + _dot(y_ref[...], wo_ref[...])
    hn = _rms(h, g_ref[...]).astype(BF16)
    acc = jnp.zeros(h.shape, F32)
    for c in range(D_FF // FF_TILE):
        u = jnp.maximum(_dot(hn, wup_ref[:, c * FF_TILE:(c + 1) * FF_TILE]), 0.0)
        acc += _dot((u * u).astype(BF16), wdn_ref[c * FF_TILE:(c + 1) * FF_TILE, :])
    h = h + acc
    if final_norm:
        h = _rms(h, gf_ref[...])
    o_ref[...] = h


def _out_mlp(x2, y2, w_out, g_mlp, w_up, w_down, g_final, final_norm):
    n = x2.shape[0]
    tm = ROW_TILE
    kin = y2.shape[1]
    row = lambda width: pl.BlockSpec((tm, width), lambda i: (i, 0))
    return pl.pallas_call(
        functools.partial(_out_mlp_kernel, final_norm=final_norm),
        grid=(n // tm,),
        in_specs=[row(D_MODEL), row(kin), _const_spec((kin, D_MODEL)), _const_spec((1, D_MODEL)),
                  _const_spec((D_MODEL, D_FF)), _const_spec((D_FF, D_MODEL)),
                  _const_spec((1, D_MODEL))],
        out_specs=row(D_MODEL),
        out_shape=jax.ShapeDtypeStruct((n, D_MODEL), F32),
        compiler_params=pltpu.CompilerParams(dimension_semantics=("parallel",),
                                             vmem_limit_bytes=VMEM_LIMIT_BYTES),
        name="out_proj_mlp_final" if final_norm else "out_proj_mlp",
    )(x2, y2, w_out.astype(BF16), g_mlp[None, :], w_up.astype(BF16), w_down.astype(BF16),
      g_final[None, :])


def _ret_in_kernel(x_ref, pos_ref, g_ref, wq_ref, wk_ref, wv_ref, wg_ref, inv_ref,
                   q_ref, k_ref, v_ref, gate_ref):
    hn = _rms(x_ref[...], g_ref[...]).astype(BF16)
    ang = pos_ref[...].astype(F32) * inv_ref[...]
    cos = jnp.cos(ang)
    sin = jnp.sin(ang)
    half = DK_R // 2

    def rot_store(w_ref, out_ref, mul):
        for h in range(H_R):
            y = _dot(hn, w_ref[:, h * DK_R:(h + 1) * DK_R])
            x1 = y[:, :half]
            x2 = y[:, half:]
            out_ref[:, h * DK_R:h * DK_R + half] = ((x1 * cos - x2 * sin) * mul).astype(BF16)
            out_ref[:, h * DK_R + half:(h + 1) * DK_R] = ((x2 * cos + x1 * sin) * mul).astype(BF16)

    rot_store(wq_ref, q_ref, 1.0)
    rot_store(wk_ref, k_ref, DK_R ** -0.5)
    for c in range(H_R):
        v_ref[:, c * DV_R:(c + 1) * DV_R] = _dot(hn, wv_ref[:, c * DV_R:(c + 1) * DV_R]).astype(BF16)
        gate_ref[:, c * DV_R:(c + 1) * DV_R] = _dot(hn, wg_ref[:, c * DV_R:(c + 1) * DV_R]).astype(BF16)


def _ret_in_proj(x2, pos2, g, w_in):
    n = x2.shape[0]
    tm = ROW_TILE
    dq = H_R * DK_R
    dv = H_R * DV_R
    wq = w_in[:, :dq].astype(BF16)
    wk = w_in[:, dq:2 * dq].astype(BF16)
    wv = w_in[:, 2 * dq:2 * dq + dv].astype(BF16)
    wg = w_in[:, 2 * dq + dv:].astype(BF16)
    half = DK_R // 2
    inv = (RET_THETA ** (-jnp.arange(half, dtype=F32) / half))[None, :]
    row = lambda width: pl.BlockSpec((tm, width), lambda i: (i, 0))
    outs = [(n, dq), (n, dq), (n, dv), (n, dv)]
    return pl.pallas_call(
        _ret_in_kernel,
        grid=(n // tm,),
        in_specs=[row(D_MODEL), row(1), _const_spec((1, D_MODEL)),
                  _const_spec(wq.shape), _const_spec(wk.shape), _const_spec(wv.shape),
                  _const_spec(wg.shape), _const_spec((1, half))],
        out_specs=[row(s[1]) for s in outs],
        out_shape=[jax.ShapeDtypeStruct(s, BF16) for s in outs],
        compiler_params=pltpu.CompilerParams(dimension_semantics=("parallel",),
                                             vmem_limit_bytes=VMEM_LIMIT_BYTES),
        name="ret_in_proj",
    )(x2, pos2, g[None, :], wq, wk, wv, wg, inv)


def _retention_kernel(decay_ref, xi_ref, zeta_ref, gc_ref, q_ref, k_ref, v_ref, gate_ref, gn_ref,
                      y_ref, state_ref):
    @pl.when(pl.program_id(1) == 0)
    def _():
        state_ref[...] = jnp.zeros(state_ref.shape, F32)

    for h in range(H_R):
        dq = slice(h * DK_R, (h + 1) * DK_R)
        dv = slice(h * DV_R, (h + 1) * DV_R)
        q = q_ref[:, dq]
        k = k_ref[:, dq]
        v = v_ref[:, dv]
        state = state_ref[h]
        inner = _dot_nt(q, k) * decay_ref[h]
        out = (_dot(inner.astype(BF16), v)
               + _dot((q.astype(F32) * xi_ref[h]).astype(BF16), state.astype(BF16)))
        state_ref[h] = (gc_ref[h][0:1, 0:1] * state
                        + _dot_tn((k.astype(F32) * zeta_ref[h]).astype(BF16), v))

        mu = jnp.mean(out, axis=-1, keepdims=True)
        d = out - mu
        var = jnp.mean(d * d, axis=-1, keepdims=True)
        yn = d * lax.rsqrt(var + RMS_EPS) * gn_ref[:, dv]
        gate = gate_ref[:, dv].astype(F32)
        y_ref[:, dv] = (gate * jax.nn.sigmoid(gate) * yn).astype(BF16)


def _retention(q, k, v, gate, gn_g):
    b, s, _ = q.shape
    c = RET_CHUNK
    log_gamma = jnp.log1p(-jnp.exp2(-5.0 - jnp.arange(H_R, dtype=F32)))
    j = jnp.arange(c, dtype=F32)
    diff = j[:, None] - j[None, :]
    decay = jnp.where(diff >= 0, jnp.exp(jnp.maximum(diff, 0.0) * log_gamma[:, None, None]), 0.0)
    xi = jnp.exp((j + 1.0) * log_gamma[:, None])[..., None]
    zeta = jnp.exp((c - 1.0 - j) * log_gamma[:, None])[..., None]
    gamma_c = jnp.broadcast_to(jnp.exp(c * log_gamma)[:, None, None], (H_R, 8, LANES))
    rows = lambda width: pl.BlockSpec((None, c, width), lambda bi, ci: (bi, ci, 0))
    return pl.pallas_call(
        _retention_kernel,
        grid=(b, s // c),
        in_specs=[_const_spec((H_R, c, c)), _const_spec((H_R, c, 1)), _const_spec((H_R, c, 1)),
                  _const_spec((H_R, 8, LANES)),
                  rows(H_R * DK_R), rows(H_R * DK_R), rows(H_R * DV_R), rows(H_R * DV_R),
                  _const_spec((1, H_R * DV_R))],
        out_specs=rows(H_R * DV_R),
        out_shape=jax.ShapeDtypeStruct((b, s, H_R * DV_R), BF16),
        scratch_shapes=[pltpu.VMEM((H_R, DK_R, DV_R), F32)],
        compiler_params=pltpu.CompilerParams(
            dimension_semantics=("parallel", "arbitrary"),
            vmem_limit_bytes=VMEM_LIMIT_BYTES),
        name="retention",
    )(decay, xi, zeta, gamma_c, q, k, v, gate, gn_g[None, :])


def kernel(x, positions, norm_mix_g, norm_mlp_g, w_in_a, w_out_a, w_in_b, ret_norm_g, w_out_b,
           w_mlp_up, w_mlp_down, final_norm_g):
    b, s, d = x.shape
    n = b * s
    x2 = x.reshape(n, d)
    pos2 = positions.reshape(n, 1)

    q, k, ksq, vt, qi, kilo, kihi, w = _dsa_in_proj(x2, pos2, norm_mix_g[0], w_in_a[0], b, s)
    r3 = lambda a: a.reshape(b, s, a.shape[-1])
    o = _dsa_attention(q, r3(qi), r3(w), r3(k), ksq, vt, r3(kilo), r3(kihi))
    h = _out_mlp(x2, o.reshape(n, -1), w_out_a[0], norm_mlp_g[0], w_mlp_up[0], w_mlp_down[0],
                 final_norm_g, final_norm=False)

    rq, rk, rv, rg = _ret_in_proj(h, pos2, norm_mix_g[1], w_in_b[0])
    y = _retention(r3(rq), r3(rk), r3(rv), r3(rg), ret_norm_g[0])
    out = _out_mlp(h, y.reshape(n, -1), w_out_b[0], norm_mlp_g[1], w_mlp_up[1], w_mlp_down[1],
                   final_norm_g, final_norm=True)
    return out.reshape(b, s, d)
```

```python
import functools
import math

import jax
import jax.numpy as jnp
import numpy as np
from jax import lax
from jax.experimental import pallas as pl
from jax.experimental.pallas import tpu as pltpu

D_MODEL = 1024
RMS_EPS = 1e-6
ROPE_THETA = 500000.0
H_A = 8
DH_A = D_MODEL // H_A
ROT_A = DH_A // 4
H_IDX = 8
D_IDX = 64
ROT_IDX = D_IDX // 4
TOPK_MAX = 256
H_R = 4
DK_R = D_MODEL // H_R
DV_R = 2 * DK_R
RET_THETA = 10000.0
D_FF = 4 * D_MODEL

LANES = 128
HALF_TILE = LANES // 2
assert DH_A == LANES and 2 * D_IDX == LANES and ROT_IDX == ROT_A // 2 and ROT_A <= HALF_TILE
VMEM_LIMIT_BYTES = 56 * 1024 * 1024

ROW_TILE = 512
FF_TILE = 1024
Q_TILE = 256
K_TILE = 512
COUNT_ROWS = 64
COUNT16_ROWS = 128
RET_CHUNK = 256

NEG_BIG = -1e30
NORM_SLACK = 1.02
UNDERFLOW_GUARD = 2.0 ** -80
Q_SCALE = math.log2(math.e) / math.sqrt(DH_A)
INT_MIN = np.int32(-2 ** 31)
KEY_NEG_INF = np.int32(-2 ** 31 + 0x7FFFFF)
MANT_MASK = np.int32(0x7FFFFFFF)

BF16 = jnp.bfloat16
F32 = jnp.float32


def _dot(a, b):
    return jnp.dot(a, b, preferred_element_type=F32)


def _dot_nt(a, b):
    return lax.dot_general(a, b, (((1,), (1,)), ((), ())), preferred_element_type=F32)


def _dot_tn(a, b):
    return lax.dot_general(a, b, (((0,), (0,)), ((), ())), preferred_element_type=F32)


def _rms(x, g):
    ms = jnp.mean(x * x, axis=-1, keepdims=True)
    return x * lax.rsqrt(ms + RMS_EPS) * g


def _const_spec(shape):
    nd = len(shape)
    return pl.BlockSpec(shape, lambda *_: (0,) * nd, pipeline_mode=pl.Buffered(1))


def _dsa_in_kernel(x_ref, pos_ref, g_ref, wq_ref, wkv_ref, wqi_ref, wki_ref, ww_ref,
                   inv_ref, sgn_ref,
                   q_ref, k_ref, ksq_ref, vt_ref, qi_ref, kilo_ref, kihi_ref, w_ref):
    hn = _rms(x_ref[...], g_ref[...]).astype(BF16)
    pos = pos_ref[...].astype(F32)

    sgn = sgn_ref[...]
    rot_lane = (lax.broadcasted_iota(jnp.int32, (1, LANES), 1) & (HALF_TILE - 1)) < ROT_A // 2
    ang = pos * inv_ref[...]
    cos_t = jnp.cos(ang)
    sin_t = jnp.sin(ang)
    cos_a = jnp.where(rot_lane, cos_t, 1.0)
    sin_a = sin_t * sgn
    cos_i = jnp.where(rot_lane, pltpu.roll(cos_t, LANES - ROT_A // 2, 1), 1.0)
    sin_i = pltpu.roll(sin_t, LANES - ROT_A // 2, 1) * sgn

    def rot_a(y):
        return y * cos_a + pltpu.roll(y, HALF_TILE, 1) * sin_a

    def rot_i(y):
        return y * cos_i + pltpu.roll(y, HALF_TILE, 1) * sin_i

    for c in range(H_A // 2):
        y = _dot(hn, wq_ref[:, c * 2 * DH_A:(c + 1) * 2 * DH_A])
        for t in range(2):
            h = 2 * c + t
            q_ref[2 * c + t] = (rot_a(y[:, t * DH_A:(t + 1) * DH_A]) * Q_SCALE).astype(BF16)

    kv = _dot(hn, wkv_ref[...])
    k_rot = rot_a(kv[:, :DH_A]).astype(BF16)
    k_ref[...] = k_rot
    k_f32 = k_rot.astype(F32)
    ksq_ref[...] = _dot_nt(jnp.ones((8, DH_A), BF16), (k_f32 * k_f32).astype(BF16))
    vt_ref[...] = kv[:, DH_A:].T.astype(BF16)

    for c in range(H_IDX * D_IDX // (2 * LANES)):
        y = _dot(hn, wqi_ref[:, c * 2 * LANES:(c + 1) * 2 * LANES])
        for t in range(2):
            j = 2 * c + t
            qi_ref[:, j * LANES:(j + 1) * LANES] = rot_i(y[:, t * LANES:(t + 1) * LANES]).astype(BF16)

    ki = _dot(hn, wki_ref[...])
    kilo_ref[...] = rot_i(ki[:, :LANES]).astype(BF16)
    kihi_ref[...] = rot_i(ki[:, LANES:]).astype(BF16)

    w_ref[...] = _dot(hn, ww_ref[...]) * (H_IDX ** -0.5 * D_IDX ** -0.5)


def _dsa_in_proj(x2, pos2, g, w_in, b, s):
    n = x2.shape[0]
    tm = ROW_TILE
    tiles_per_seq = s // tm
    o_q = H_A * DH_A
    o_k = o_q + DH_A
    o_v = o_k + DH_A
    o_qi = o_v + H_IDX * D_IDX
    o_ki = o_qi + D_IDX
    ha = ROT_A // 2
    cat = lambda parts: jnp.concatenate(parts, axis=-1)

    def head_order(w):
        return cat([w[..., :ha], w[..., ROT_A:ROT_A + HALF_TILE - ha],
                    w[..., ha:ROT_A], w[..., ROT_A + HALF_TILE - ha:]])

    wq = head_order(w_in[:, :o_q].reshape(D_MODEL, H_A, DH_A)).reshape(D_MODEL, o_q).astype(BF16)
    wkv = cat([head_order(w_in[:, o_q:o_k]), w_in[:, o_k:o_v]]).astype(BF16)

    hi_ = ROT_IDX // 2

    def pair_order(a, b):
        return cat([a[..., :hi_], b[..., :hi_], a[..., ROT_IDX:],
                    a[..., hi_:ROT_IDX], b[..., hi_:ROT_IDX], b[..., ROT_IDX:]])

    wqi_heads = w_in[:, o_v:o_qi].reshape(D_MODEL, H_IDX // 2, 2, D_IDX)
    wqi = (pair_order(wqi_heads[:, :, 0], wqi_heads[:, :, 1])
           .reshape(D_MODEL, H_IDX * D_IDX).astype(BF16))
    wki_cols = w_in[:, o_qi:o_ki]
    none = jnp.zeros_like(wki_cols)
    wki = cat([pair_order(wki_cols, none), pair_order(none, wki_cols)]).astype(BF16)
    ww = jnp.pad(w_in[:, o_ki:], ((0, 0), (0, LANES - H_IDX))).astype(BF16)

    inv_of = lambda rot: ROPE_THETA ** (-jnp.arange(rot // 2, dtype=F32) / (rot // 2))
    half_row = jnp.concatenate([inv_of(ROT_A), inv_of(ROT_IDX), inv_of(ROT_IDX),
                                jnp.zeros((HALF_TILE - ROT_A,), F32)])
    inv = jnp.concatenate([half_row, half_row])[None, :]
    n_rot = 2 * hi_
    sgn = jnp.concatenate([-jnp.ones((n_rot,), F32), jnp.zeros((HALF_TILE - n_rot,), F32),
                           jnp.ones((n_rot,), F32), jnp.zeros((HALF_TILE - n_rot,), F32)])[None, :]

    row = lambda width: pl.BlockSpec((tm, width), lambda i: (i, 0))
    rows_out = lambda width, dt: (row(width), jax.ShapeDtypeStruct((n, width), dt))
    vt_out = (pl.BlockSpec((None, DH_A, tm), lambda i: (i // tiles_per_seq, 0, i % tiles_per_seq)),
              jax.ShapeDtypeStruct((b, DH_A, s), BF16))
    q_out = (pl.BlockSpec((None, H_A, tm, DH_A),
                          lambda i: (i // tiles_per_seq, 0, i % tiles_per_seq, 0)),
             jax.ShapeDtypeStruct((b, H_A, s, DH_A), BF16))
    ksq_out = (pl.BlockSpec((None, 8, tm), lambda i: (i // tiles_per_seq, 0, i % tiles_per_seq)),
               jax.ShapeDtypeStruct((b, 8, s), F32))
    outs = [q_out, rows_out(DH_A, BF16), ksq_out, vt_out, rows_out(H_IDX * D_IDX, BF16),
            rows_out(LANES, BF16), rows_out(LANES, BF16), rows_out(LANES, F32)]
    return pl.pallas_call(
        _dsa_in_kernel,
        grid=(n // tm,),
        in_specs=[row(D_MODEL), row(1), _const_spec((1, D_MODEL)),
                  _const_spec(wq.shape), _const_spec(wkv.shape), _const_spec(wqi.shape),
                  _const_spec(wki.shape), _const_spec(ww.shape),
                  _const_spec((1, LANES)), _const_spec((1, LANES))],
        out_specs=[o[0] for o in outs],
        out_shape=[o[1] for o in outs],
        compiler_params=pltpu.CompilerParams(dimension_semantics=("parallel",),
                                             vmem_limit_bytes=VMEM_LIMIT_BYTES),
        name="dsa_in_proj",
    )(x2, pos2, g[None, :], wq, wkv, wqi, wki, ww, inv, sgn)


def _key_to_float(key):
    key = jnp.maximum(key, KEY_NEG_INF)
    bits = jnp.where(key >= 0, key, key ^ MANT_MASK)
    return lax.bitcast_convert_type(bits, F32)


def _sum_row_groups(x, n_acc=4, rows=8):
    groups = [x[g * rows:(g + 1) * rows, :] for g in range(x.shape[0] // rows)]
    accs = groups[:n_acc]
    for i, g in enumerate(groups[n_acc:]):
        accs[i % n_acc] = accs[i % n_acc] + g
    while len(accs) > 1:
        accs = [a + b for a, b in zip(accs[::2], accs[1::2])]
    return accs[0]


def _dsa_attn_kernel(q_ref, qi_ref, w_ref, k_ref, ksq_ref, vt_ref, kilo_ref, kihi_ref, o_ref,
                     sc_ref, sc16_ref, m_ref, l_ref, acc_ref, *, topk):
    tq = qi_ref.shape[0]
    q0 = pl.program_id(1) * tq
    n_ck = lax.div(q0 + (tq + K_TILE - 1), jnp.int32(K_TILE))
    query_t = q0 + lax.broadcasted_iota(jnp.int32, (1, tq), 1)
    key_i = lax.broadcasted_iota(jnp.int32, (K_TILE, 1), 0)
    w_t = w_ref[...].T

    def for_each_key_tile(body):
        def pair(jj, carry):
            body(2 * jj)
            body(2 * jj + 1)
            return carry

        lax.fori_loop(0, lax.div(n_ck, jnp.int32(2)), pair, 0)

        @pl.when(lax.rem(n_ck, jnp.int32(2)) == 1)
        def _():
            body(n_ck - 1)

    def score_body(j):
        off = pl.multiple_of(j * K_TILE, K_TILE)
        klo = kilo_ref[pl.ds(off, K_TILE), :]
        khi = kihi_ref[pl.ds(off, K_TILE), :]
        acc = jnp.zeros((K_TILE, tq), F32)
        for p in range(H_IDX // 2):
            a = qi_ref[:, p * LANES:(p + 1) * LANES]
            acc += jnp.maximum(_dot_nt(klo, a), 0.0) * w_t[2 * p:2 * p + 1, :]
            acc += jnp.maximum(_dot_nt(khi, a), 0.0) * w_t[2 * p + 1:2 * p + 2, :]
        sc = jnp.where(off + key_i <= query_t, acc, -jnp.inf)
        sc_ref[pl.ds(off, K_TILE), :] = sc
        sc16_ref[pl.ds(off, K_TILE), :] = sc.astype(BF16)

    for_each_key_tile(score_body)

    def count_ge(thr):
        def body(j, part):
            off = pl.multiple_of(j * K_TILE, K_TILE)
            for r in range(K_TILE // COUNT_ROWS):
                sc = sc_ref[pl.ds(off + r * COUNT_ROWS, COUNT_ROWS), :]
                part = part + _sum_row_groups(jnp.where(sc >= thr, 1.0, 0.0))
            return part

        part = lax.fori_loop(0, n_ck, body, jnp.zeros((8, tq), F32))
        return jnp.sum(part, axis=0, keepdims=True)

    def count_ge16(thr16):
        one = jnp.ones((), BF16)
        zero = jnp.zeros((), BF16)

        def body(j, part):
            off = pl.multiple_of(j * K_TILE, K_TILE)
            p16 = None
            for r in range(K_TILE // COUNT16_ROWS):
                sc = sc16_ref[pl.ds(off + r * COUNT16_ROWS, COUNT16_ROWS), :]
                g = _sum_row_groups(jnp.where(sc >= thr16, one, zero), n_acc=2, rows=16)
                p16 = g if p16 is None else p16 + g
            return part + p16.astype(F32)

        part = lax.fori_loop(0, n_ck, body, jnp.zeros((16, tq), F32))
        return jnp.sum(part, axis=0, keepdims=True)

    def bf16_key(hi):
        key = lax.shift_left(hi, 16) ^ INT_MIN
        return jnp.where(key < 0, key | 0xFFFF, key)

    def coarse_body(it, hi):
        cand = hi | lax.shift_left(jnp.int32(1), 15 - it)
        cnt = count_ge16(_key_to_float(bf16_key(cand)).astype(BF16))
        return jnp.where(cnt >= float(topk), cand, hi)

    hi = lax.fori_loop(0, 16, coarse_body, jnp.zeros((1, tq), jnp.int32))
    base = bf16_key(hi) - 32768

    def search_body(it, carry):
        d, cnt_out = carry
        cand = d | lax.shift_left(jnp.int32(1), 16 - it)
        cnt = count_ge(_key_to_float(base + cand))
        ok = cnt >= float(topk)
        return (jnp.where(ok, cand, d), jnp.where(ok, cnt_out, cnt))

    d, cnt_gt = lax.fori_loop(0, 17, search_body,
                              (jnp.zeros((1, tq), jnp.int32), jnp.zeros((1, tq), F32)))
    key = base + d
    thr = _key_to_float(key)
    cnt_ge = count_ge(thr)
    thr_up = _key_to_float(key + 1)
    need = float(topk) - cnt_gt

    excess = jnp.where((cnt_ge - cnt_gt > need) & (key > KEY_NEG_INF), 1, 0)

    @pl.when(jnp.max(excess) > 0)
    def _():
        up_to = jnp.where(lax.broadcasted_iota(jnp.int32, (K_TILE, K_TILE), 1)
                          <= lax.broadcasted_iota(jnp.int32, (K_TILE, K_TILE), 0), 1.0, 0.0).astype(BF16)

        def body(j, seen):
            off = pl.multiple_of(j * K_TILE, K_TILE)
            sc = sc_ref[pl.ds(off, K_TILE), :]
            tie = (sc >= thr) & jnp.logical_not(sc >= thr_up)
            rank = _dot(up_to, jnp.where(tie, 1.0, 0.0).astype(BF16)) + seen
            sc_ref[pl.ds(off, K_TILE), :] = jnp.where(tie & (rank > need), -jnp.inf, sc)
            return rank[K_TILE - 1:K_TILE, :]

        lax.fori_loop(0, n_ck, body, jnp.zeros((1, tq), F32))

    def key_tile(j):
        off = pl.multiple_of(j * K_TILE, K_TILE)
        sc = sc_ref[pl.ds(off, K_TILE), :]
        bias = jnp.where((sc >= thr) & (off + key_i <= query_t), 0.0, NEG_BIG)
        q_all = q_ref[...].reshape(H_A * tq, DH_A)
        lg_all = _dot_nt(k_ref[pl.ds(off, K_TILE), :], q_all)
        return bias, lg_all, vt_ref[:, pl.ds(off, K_TILE)]

    q_f32 = q_ref[...].reshape(H_A * tq, DH_A).astype(F32)
    q_sq = _dot_nt(jnp.ones((8, DH_A), BF16), (q_f32 * q_f32).astype(BF16))[0:1, :]
    k_sq_max = jnp.max(jnp.max(ksq_ref[...], axis=1, keepdims=True), axis=0, keepdims=True)
    shift = jnp.sqrt(q_sq * k_sq_max) * NORM_SLACK
    l_ref[...] = jnp.zeros(l_ref.shape, F32)
    acc_ref[...] = jnp.zeros(acc_ref.shape, F32)

    def bounded_body(j):
        bias, lg_all, vtc = key_tile(j)
        ps, sums = [], []
        for h in range(H_A):
            cols = slice(h * tq, (h + 1) * tq)
            p = jnp.exp2(lg_all[:, cols] - shift[:, cols] + bias)
            sums.append(jnp.sum(p, axis=0, keepdims=True))
            ps.append(p.astype(BF16))
        l_ref[...] += jnp.concatenate(sums, axis=1)
        acc_ref[...] += _dot(vtc, jnp.concatenate(ps, axis=1))

    for_each_key_tile(bounded_body)

    @pl.when(jnp.min(l_ref[...]) < UNDERFLOW_GUARD)
    def _():
        m_ref[...] = jnp.full(m_ref.shape, NEG_BIG, F32)
        l_ref[...] = jnp.zeros(l_ref.shape, F32)
        acc_ref[...] = jnp.zeros(acc_ref.shape, F32)

        def online_body(j, carry):
            bias, lg_all, vtc = key_tile(j)
            m_prev = m_ref[...]
            l_prev = l_ref[...]
            ps, m_news, l_news = [], [], []
            for h in range(H_A):
                cols = slice(h * tq, (h + 1) * tq)
                lg = lg_all[:, cols] + bias
                m_new = jnp.maximum(m_prev[:, cols], jnp.max(lg, axis=0, keepdims=True))
                p = jnp.exp2(lg - m_new)
                l_news.append(jnp.exp2(m_prev[:, cols] - m_new) * l_prev[:, cols]
                              + jnp.sum(p, axis=0, keepdims=True))
                m_news.append(m_new)
                ps.append(p.astype(BF16))
            m_new = jnp.concatenate(m_news, axis=1)
            l_ref[...] = jnp.concatenate(l_news, axis=1)
            acc_ref[...] = (jnp.exp2(m_prev - m_new) * acc_ref[...]
                            + _dot(vtc, jnp.concatenate(ps, axis=1)))
            m_ref[...] = m_new
            return carry

        lax.fori_loop(0, n_ck, online_body, 0)

    out_t = acc_ref[...] / l_ref[...]
    for h in range(H_A):
        o_ref[:, h * DH_A:(h + 1) * DH_A] = out_t[:, h * tq:(h + 1) * tq].T.astype(BF16)


def _dsa_attention(q, qi, w, k, ksq, vt, kilo, kihi):
    b, s, _ = k.shape
    tq = Q_TILE
    topk = min(TOPK_MAX, s // 4)
    qt = lambda width: pl.BlockSpec((None, tq, width), lambda bi, qi_: (bi, qi_, 0))
    full = lambda width: pl.BlockSpec((None, s, width), lambda bi, qi_: (bi, 0, 0))
    return pl.pallas_call(
        functools.partial(_dsa_attn_kernel, topk=topk),
        grid=(b, s // tq),
        in_specs=[pl.BlockSpec((None, H_A, tq, DH_A), lambda bi, qi_: (bi, 0, qi_, 0)),
                  qt(H_IDX * D_IDX), qt(LANES), full(DH_A),
                  pl.BlockSpec((None, 8, s), lambda bi, qi_: (bi, 0, 0)),
                  pl.BlockSpec((None, DH_A, s), lambda bi, qi_: (bi, 0, 0)),
                  full(LANES), full(LANES)],
        out_specs=qt(H_A * DH_A),
        out_shape=jax.ShapeDtypeStruct((b, s, H_A * DH_A), BF16),
        scratch_shapes=[pltpu.VMEM((s, tq), F32),
                        pltpu.VMEM((s, tq), BF16),
                        pltpu.VMEM((1, H_A * tq), F32),
                        pltpu.VMEM((1, H_A * tq), F32),
                        pltpu.VMEM((DH_A, H_A * tq), F32)],
        compiler_params=pltpu.CompilerParams(dimension_semantics=("parallel", "parallel"),
                                             vmem_limit_bytes=VMEM_LIMIT_BYTES),
        name="dsa_attention",
    )(q, qi, w, k, ksq, vt, kilo, kihi)


def _out_mlp_kernel(x_ref, y_ref, wo_ref, g_ref, wup_ref, wdn_ref, gf_ref, o_ref, *, final_norm):
    h = x_ref[...] + _dot(y_ref[...], wo_ref[...])
    hn = _rms(h, g_ref[...]).astype(BF16)
    acc = jnp.zeros(h.shape, F32)
    for c in range(D_FF // FF_TILE):
        u = jnp.maximum(_dot(hn, wup_ref[:, c * FF_TILE:(c + 1) * FF_TILE]), 0.0)
        acc += _dot((u * u).astype(BF16), wdn_ref[c * FF_TILE:(c + 1) * FF_TILE, :])
    h = h + acc
    if final_norm:
        h = _rms(h, gf_ref[...])
    o_ref[...] = h


def _out_mlp(x2, y2, w_out, g_mlp, w_up, w_down, g_final, final_norm):
    n = x2.shape[0]
    tm = ROW_TILE
    kin = y2.shape[1]
    row = lambda width: pl.BlockSpec((tm, width), lambda i: (i, 0))
    return pl.pallas_call(
        functools.partial(_out_mlp_kernel, final_norm=final_norm),
        grid=(n // tm,),
        in_specs=[row(D_MODEL), row(kin), _const_spec((kin, D_MODEL)), _const_spec((1, D_MODEL)),
                  _const_spec((D_MODEL, D_FF)), _const_spec((D_FF, D_MODEL)),
                  _const_spec((1, D_MODEL))],
        out_specs=row(D_MODEL),
        out_shape=jax.ShapeDtypeStruct((n, D_MODEL), F32),
        compiler_params=pltpu.CompilerParams(dimension_semantics=("parallel",),
                                             vmem_limit_bytes=VMEM_LIMIT_BYTES),
        name="out_proj_mlp_final" if final_norm else "out_proj_mlp",
    )(x2, y2, w_out.astype(BF16), g_mlp[None, :], w_up.astype(BF16), w_down.astype(BF16),
      g_final[None, :])


def _ret_in_kernel(x_ref, pos_ref, g_ref, wq_ref, wk_ref, wv_ref, wg_ref, inv_ref,
                   q_ref, k_ref, v_ref, gate_ref):
    hn = _rms(x_ref[...], g_ref[...]).astype(BF16)
    ang = pos_ref[...].astype(F32) * inv_ref[...]
    cos = jnp.cos(ang)
    sin = jnp.sin(ang)
    half = DK_R // 2

    def rot_store(w_ref, out_ref, mul):
        for h in range(H_R):
            y = _dot(hn, w_ref[:, h * DK_R:(h + 1) * DK_R])
            x1 = y[:, :half]
            x2 = y[:, half:]
            out_ref[:, h * DK_R:h * DK_R + half] = ((x1 * cos - x2 * sin) * mul).astype(BF16)
            out_ref[:, h * DK_R + half:(h + 1) * DK_R] = ((x2 * cos + x1 * sin) * mul).astype(BF16)

    rot_store(wq_ref, q_ref, 1.0)
    rot_store(wk_ref, k_ref, DK_R ** -0.5)
    for c in range(H_R):
        v_ref[:, c * DV_R:(c + 1) * DV_R] = _dot(hn, wv_ref[:, c * DV_R:(c + 1) * DV_R]).astype(BF16)
        gate_ref[:, c * DV_R:(c + 1) * DV_R] = _dot(hn, wg_ref[:, c * DV_R:(c + 1) * DV_R]).astype(BF16)


def _ret_in_proj(x2, pos2, g, w_in):
    n = x2.shape[0]
    tm = ROW_TILE
    dq = H_R * DK_R
    dv = H_R * DV_R
    wq = w_in[:, :dq].astype(BF16)
    wk = w_in[:, dq:2 * dq].astype(BF16)
    wv = w_in[:, 2 * dq:2 * dq + dv].astype(BF16)
    wg = w_in[:, 2 * dq + dv:].astype(BF16)
    half = DK_R // 2
    inv = (RET_THETA ** (-jnp.arange(half, dtype=F32) / half))[None, :]
    row = lambda width: pl.BlockSpec((tm, width), lambda i: (i, 0))
    outs = [(n, dq), (n, dq), (n, dv), (n, dv)]
    return pl.pallas_call(
        _ret_in_kernel,
        grid=(n // tm,),
        in_specs=[row(D_MODEL), row(1), _const_spec((1, D_MODEL)),
                  _const_spec(wq.shape), _const_spec(wk.shape), _const_spec(wv.shape),
                  _const_spec(wg.shape), _const_spec((1, half))],
        out_specs=[row(s[1]) for s in outs],
        out_shape=[jax.ShapeDtypeStruct(s, BF16) for s in outs],
        compiler_params=pltpu.CompilerParams(dimension_semantics=("parallel",),
                                             vmem_limit_bytes=VMEM_LIMIT_BYTES),
        name="ret_in_proj",
    )(x2, pos2, g[None, :], wq, wk, wv, wg, inv)


def _retention_kernel(decay_ref, xi_ref, zeta_ref, gc_ref, q_ref, k_ref, v_ref, gate_ref, gn_ref,
                      y_ref, state_ref):
    @pl.when(pl.program_id(1) == 0)
    def _():
        state_ref[...] = jnp.zeros(state_ref.shape, F32)

    for h in range(H_R):
        dq = slice(h * DK_R, (h + 1) * DK_R)
        dv = slice(h * DV_R, (h + 1) * DV_R)
        q = q_ref[:, dq]
        k = k_ref[:, dq]
        v = v_ref[:, dv]
        state = state_ref[h]
        inner = _dot_nt(q, k) * decay_ref[h]
        out = (_dot(inner.astype(BF16), v)
               + _dot((q.astype(F32) * xi_ref[h]).astype(BF16), state.astype(BF16)))
        state_ref[h] = (gc_ref[h][0:1, 0:1] * state
                        + _dot_tn((k.astype(F32) * zeta_ref[h]).astype(BF16), v))

        mu = jnp.mean(out, axis=-1, keepdims=True)
        d = out - mu
        var = jnp.mean(d * d, axis=-1, keepdims=True)
        yn = d * lax.rsqrt(var + RMS_EPS) * gn_ref[:, dv]
        gate = gate_ref[:, dv].astype(F32)
        y_ref[:, dv] = (gate * jax.nn.sigmoid(gate) * yn).astype(BF16)


def _retention(q, k, v, gate, gn_g):
    b, s, _ = q.shape
    c = RET_CHUNK
    log_gamma = jnp.log1p(-jnp.exp2(-5.0 - jnp.arange(H_R, dtype=F32)))
    j = jnp.arange(c, dtype=F32)
    diff = j[:, None] - j[None, :]
    decay = jnp.where(diff >= 0, jnp.exp(jnp.maximum(diff, 0.0) * log_gamma[:, None, None]), 0.0)
    xi = jnp.exp((j + 1.0) * log_gamma[:, None])[..., None]
    zeta = jnp.exp((c - 1.0 - j) * log_gamma[:, None])[..., None]
    gamma_c = jnp.broadcast_to(jnp.exp(c * log_gamma)[:, None, None], (H_R, 8, LANES))
    rows = lambda width: pl.BlockSpec((None, c, width), lambda bi, ci: (bi, ci, 0))
    return pl.pallas_call(
        _retention_kernel,
        grid=(b, s // c),
        in_specs=[_const_spec((H_R, c, c)), _const_spec((H_R, c, 1)), _const_spec((H_R, c, 1)),
                  _const_spec((H_R, 8, LANES)),
                  rows(H_R * DK_R), rows(H_R * DK_R), rows(H_R * DV_R), rows(H_R * DV_R),
                  _const_spec((1, H_R * DV_R))],
        out_specs=rows(H_R * DV_R),
        out_shape=jax.ShapeDtypeStruct((b, s, H_R * DV_R), BF16),
        scratch_shapes=[pltpu.VMEM((H_R, DK_R, DV_R), F32)],
        compiler_params=pltpu.CompilerParams(
            dimension_semantics=("parallel", "arbitrary"),
            vmem_limit_bytes=VMEM_LIMIT_BYTES),
        name="retention",
    )(decay, xi, zeta, gamma_c, q, k, v, gate, gn_g[None, :])


def kernel(x, positions, norm_mix_g, norm_mlp_g, w_in_a, w_out_a, w_in_b, ret_norm_g, w_out_b,
           w_mlp_up, w_mlp_down, final_norm_g):
    b, s, d = x.shape
    n = b * s
    x2 = x.reshape(n, d)
    pos2 = positions.reshape(n, 1)

    q, k, ksq, vt, qi, kilo, kihi, w = _dsa_in_proj(x2, pos2, norm_mix_g[0], w_in_a[0], b, s)
    r3 = lambda a: a.reshape(b, s, a.shape[-1])
    o = _dsa_attention(q, r3(qi), r3(w), r3(k), ksq, vt, r3(kilo), r3(kihi))
    h = _out_mlp(x2, o.reshape(n, -1), w_out_a[0], norm_mlp_g[0], w_mlp_up[0], w_mlp_down[0],
                 final_norm_g, final_norm=False)

    rq, rk, rv, rg = _ret_in_proj(h, pos2, norm_mix_g[1], w_in_b[0])
    y = _retention(r3(rq), r3(rk), r3(rv), r3(rg), ret_norm_g[0])
    out = _out_mlp(h, y.reshape(n, -1), w_out_b[0], norm_mlp_g[1], w_mlp_up[1], w_mlp_down[1],
                   final_norm_g, final_norm=True)
    return out.reshape(b, s, d)
```
